```python
import math
import jax, jax.numpy as jnp
from jax import lax
import numpy as np

D_MODEL = 4096
BATCH = 1
SEQ = 8192
DEPTH = 2
DEC_BATCH = 8
DEC_SEQ = 2048
PAST_LEN = 128

HEAD_DIM = 128
N_ATTN_HEADS = 16
N_GMLP_GROUPS = 16
ATTN_WIDTH = N_ATTN_HEADS * HEAD_DIM
GMLP_WIDTH = N_GMLP_GROUPS * HEAD_DIM
MIX_WIDTH = ATTN_WIDTH + GMLP_WIDTH
IN_WIDTH = 3 * ATTN_WIDTH + 2 * GMLP_WIDTH
GMLP_CHUNK = 128
GRID_W = 64
WIN_H = 8
WIN_W = 16
N_EXPERTS = 64
TOP_K = 8
N_EXPERT_GROUPS = 8
TOPK_GROUPS = 4
EXPERT_FF = 128
SHARED_FF = 1024
ROUTED_SCALE = 2.5
PLE_DIM = 256
ALPHA = (2 * DEPTH) ** 0.25
BETA = (8 * DEPTH) ** -0.25
LN_EPS = 1e-5
NEG_INF = -1e30

kernel_name = 'hybrid_na2d_gmlp_moe_encoder'


def layer_norm(x, g, b):
    x32 = x.astype(jnp.float32)
    mu = jnp.mean(x32, axis=-1, keepdims=True)
    var = jnp.mean(jnp.square(x32 - mu), axis=-1, keepdims=True)
    y = (x32 - mu) * lax.rsqrt(var + LN_EPS) * g.astype(jnp.float32) + b.astype(jnp.float32)
    return y.astype(x.dtype)


def rms_norm(x):
    x32 = x.astype(jnp.float32)
    return (x32 * lax.rsqrt(jnp.mean(jnp.square(x32), axis=-1, keepdims=True) + LN_EPS)).astype(x.dtype)


def neighbourhood_attention(q, k, v, rpb):
    B, T, H, Dh = q.shape
    R = T // GRID_W
    kh = min(WIN_H, R)
    qg = (q * Dh ** -0.5).reshape(B, R, GRID_W, H, Dh)
    kg = k.reshape(B, R, GRID_W, H, Dh)
    vg = v.reshape(B, R, GRID_W, H, Dh)
    cols = np.arange(GRID_W)
    col_start = np.clip(cols - WIN_W // 2, 0, GRID_W - WIN_W)
    col_mask = (cols[None, :] >= col_start[:, None]) & (cols[None, :] < col_start[:, None] + WIN_W)
    col_off = np.clip(cols[None, :] - cols[:, None] + (WIN_W - 1), 0, 2 * WIN_W - 2)
    rpb_c = jnp.where(col_mask[None, None], rpb[:, :, col_off].astype(jnp.float32),
                      jnp.float32(NEG_INF))

    def row_block(args):
        r, q_r = args
        rs = jnp.clip(r - kh // 2, 0, R - kh)
        k_rows = lax.dynamic_slice_in_dim(kg, rs, kh, axis=1)
        v_rows = lax.dynamic_slice_in_dim(vg, rs, kh, axis=1)
        row_off = rs + jnp.arange(kh) - r + (WIN_H - 1)
        bias = jnp.take(rpb_c, row_off, axis=1).transpose(0, 2, 1, 3)
        s = jnp.einsum('bchd,bijhd->bhcij', q_r, k_rows).astype(jnp.float32) + bias[None]
        p = jax.nn.softmax(s.reshape(B, H, GRID_W, kh * GRID_W), axis=-1).reshape(s.shape).astype(v.dtype)
        return jnp.einsum('bhcij,bijhd->bchd', p, v_rows)

    out = lax.map(row_block, (jnp.arange(R), qg.swapaxes(0, 1)))
    return out.swapaxes(0, 1).reshape(B, T, H * Dh)


def spatial_gating(u, v, ln_g, ln_b, w_s, b_s):
    B, T, _ = u.shape
    N = T // GMLP_CHUNK
    shp = (B, N, GMLP_CHUNK, N_GMLP_GROUPS, HEAD_DIM)
    vn = layer_norm(v.reshape(shp), ln_g.reshape(N_GMLP_GROUPS, HEAD_DIM), ln_b.reshape(N_GMLP_GROUPS, HEAD_DIM))
    mixed = jnp.einsum('gts,bnsgd->bntgd', w_s, vn) + b_s.T[:, :, None]
    return (u.reshape(shp) * mixed).reshape(B, T, GMLP_WIDTH)


def moe(h, w_router, router_bias, w_e_gate, w_e_up, w_e_down, w_sh_gate, w_sh_up, w_sh_down):
    B, T, D = h.shape
    xt = h.reshape(B * T, D)
    n = B * T
    scores = jax.nn.sigmoid((xt @ w_router).astype(jnp.float32))
    sel = scores + router_bias.astype(jnp.float32)
    grp = sel.reshape(n, N_EXPERT_GROUPS, N_EXPERTS // N_EXPERT_GROUPS)
    grp_score = jnp.sum(lax.top_k(grp, 2)[0], axis=-1)
    _, gidx = lax.top_k(grp_score, TOPK_GROUPS)
    gmask = jnp.sum(jax.nn.one_hot(gidx, N_EXPERT_GROUPS, dtype=jnp.float32), axis=1)
    emask = jnp.repeat(gmask, N_EXPERTS // N_EXPERT_GROUPS, axis=1)
    _, eidx = lax.top_k(jnp.where(emask > 0, sel, -jnp.inf), TOP_K)
    w = jnp.take_along_axis(scores, eidx, axis=1)
    w = w / jnp.sum(w, axis=-1, keepdims=True) * ROUTED_SCALE
    gates = jnp.einsum('nk,nke->ne', w, jax.nn.one_hot(eidx, N_EXPERTS, dtype=jnp.float32)).astype(h.dtype)
    act = jax.nn.silu(jnp.einsum('nd,edf->nef', xt, w_e_gate)) * jnp.einsum('nd,edf->nef', xt, w_e_up)
    routed = jnp.einsum('nef,efd->nd', act * gates[:, :, None], w_e_down)
    shared = (jax.nn.silu(xt @ w_sh_gate) * (xt @ w_sh_up)) @ w_sh_down
    return (routed + shared).reshape(B, T, D)


def encoder_layer(x, p, w_in, attn_rpb, gmlp_ln_g, gmlp_ln_b, gmlp_ws, gmlp_bs, mix_norm_g, w_out,
                  ln1_g, ln1_b, w_router, router_bias, w_e_gate, w_e_up, w_e_down,
                  w_sh_gate, w_sh_up, w_sh_down, ln2_g, ln2_b, w_ple, w_ple_gate):
    B, T, _ = x.shape
    proj = x @ w_in
    q, k, v, gu, gv = jnp.split(proj, [ATTN_WIDTH, 2 * ATTN_WIDTH, 3 * ATTN_WIDTH, 3 * ATTN_WIDTH + GMLP_WIDTH], axis=-1)
    hs = (B, T, N_ATTN_HEADS, HEAD_DIM)
    a = neighbourhood_attention(q.reshape(hs), k.reshape(hs), v.reshape(hs), attn_rpb)
    g = spatial_gating(jax.nn.gelu(gu), jax.nn.gelu(gv), gmlp_ln_g, gmlp_ln_b, gmlp_ws, gmlp_bs)
    mixed = jnp.concatenate([rms_norm(a), rms_norm(g)], axis=-1) * mix_norm_g
    h = layer_norm(ALPHA * x + mixed @ w_out, ln1_g, ln1_b)
    h2 = layer_norm(ALPHA * h + moe(h, w_router, router_bias, w_e_gate, w_e_up, w_e_down,
                                   w_sh_gate, w_sh_up, w_sh_down), ln2_g, ln2_b)
    return h2 + jax.nn.sigmoid(h2 @ w_ple_gate) * (p @ w_ple)


def setup_inputs(seed: int = 0) -> dict:
    key = jax.random.key(seed)
    ks = jax.random.split(key, 26)
    f32 = jnp.float32
    nrm = lambda k, s, sc: jax.random.normal(k, s, f32) * sc
    col_scale = jnp.concatenate([jnp.ones((2 * ATTN_WIDTH,), f32), jnp.full((ATTN_WIDTH,), BETA, f32),
                                 jnp.ones((2 * GMLP_WIDTH,), f32)])
    return {
        'x_prompt': nrm(ks[0], (BATCH, SEQ, D_MODEL), 1.0),
        'x_sample': nrm(ks[1], (DEC_BATCH, DEC_SEQ, D_MODEL), 1.0),
        'p_prompt': nrm(ks[2], (DEPTH, BATCH, SEQ, PLE_DIM), 1.0),
        'p_sample': nrm(ks[3], (DEPTH, DEC_BATCH, DEC_SEQ, PLE_DIM), 1.0),
        'w_in': nrm(ks[4], (DEPTH, D_MODEL, IN_WIDTH), D_MODEL ** -0.5) * col_scale,
        'attn_rpb': nrm(ks[5], (DEPTH, N_ATTN_HEADS, 2 * WIN_H - 1, 2 * WIN_W - 1), 0.1),
        'gmlp_ln_g': 1.0 + nrm(ks[6], (DEPTH, GMLP_WIDTH), 0.01),
        'gmlp_ln_b': nrm(ks[7], (DEPTH, GMLP_WIDTH), 0.01),
        'gmlp_ws': nrm(ks[8], (DEPTH, N_GMLP_GROUPS, GMLP_CHUNK, GMLP_CHUNK), GMLP_CHUNK ** -0.5),
        'gmlp_bs': 1.0 + nrm(ks[9], (DEPTH, N_GMLP_GROUPS, GMLP_CHUNK), 0.01),
        'mix_norm_g': 1.0 + nrm(ks[10], (DEPTH, MIX_WIDTH), 0.01),
        'w_out': nrm(ks[11], (DEPTH, MIX_WIDTH, D_MODEL), MIX_WIDTH ** -0.5 * BETA),
        'ln1_g': 1.0 + nrm(ks[12], (DEPTH, D_MODEL), 0.01),
        'ln1_b': nrm(ks[13], (DEPTH, D_MODEL), 0.01),
        'w_router': nrm(ks[14], (DEPTH, D_MODEL, N_EXPERTS), D_MODEL ** -0.5),
        'router_bias': nrm(ks[15], (DEPTH, N_EXPERTS), 0.01),
        'w_e_gate': nrm(ks[16], (DEPTH, N_EXPERTS, D_MODEL, EXPERT_FF), D_MODEL ** -0.5),
        'w_e_up': nrm(ks[17], (DEPTH, N_EXPERTS, D_MODEL, EXPERT_FF), D_MODEL ** -0.5),
        'w_e_down': nrm(ks[18], (DEPTH, N_EXPERTS, EXPERT_FF, D_MODEL), EXPERT_FF ** -0.5 * BETA),
        'w_sh_gate': nrm(ks[19], (DEPTH, D_MODEL, SHARED_FF), D_MODEL ** -0.5),
        'w_sh_up': nrm(ks[20], (DEPTH, D_MODEL, SHARED_FF), D_MODEL ** -0.5),
        'w_sh_down': nrm(ks[21], (DEPTH, SHARED_FF, D_MODEL), SHARED_FF ** -0.5 * BETA),
        'ln2_g': 1.0 + nrm(ks[22], (DEPTH, D_MODEL), 0.01),
        'ln2_b': nrm(ks[23], (DEPTH, D_MODEL), 0.01),
        'w_ple': nrm(ks[24], (DEPTH, PLE_DIM, D_MODEL), PLE_DIM ** -0.5 * BETA),
        'w_ple_gate': nrm(ks[25], (DEPTH, D_MODEL, D_MODEL), D_MODEL ** -0.5),
    }


def reference(x_prompt, x_sample, p_prompt, p_sample, w_in, attn_rpb, gmlp_ln_g, gmlp_ln_b, gmlp_ws, gmlp_bs,
              mix_norm_g, w_out, ln1_g, ln1_b, w_router, router_bias, w_e_gate, w_e_up, w_e_down,
              w_sh_gate, w_sh_up, w_sh_down, ln2_g, ln2_b, w_ple, w_ple_gate):
    y_prompt = x_prompt
    y_sample = x_sample
    for i in range(DEPTH):
        lw = (w_in[i], attn_rpb[i], gmlp_ln_g[i], gmlp_ln_b[i], gmlp_ws[i], gmlp_bs[i], mix_norm_g[i], w_out[i],
              ln1_g[i], ln1_b[i], w_router[i], router_bias[i], w_e_gate[i], w_e_up[i], w_e_down[i],
              w_sh_gate[i], w_sh_up[i], w_sh_down[i], ln2_g[i], ln2_b[i], w_ple[i], w_ple_gate[i])
        y_prompt = encoder_layer(y_prompt, p_prompt[i], *lw)
        y_sample = encoder_layer(y_sample, p_sample[i], *lw)
    return (y_prompt, y_sample)
```

```python
import functools

import numpy as np
import jax
import jax.numpy as jnp
from jax import lax
from jax.experimental import pallas as pl
from jax.experimental.pallas import tpu as pltpu

_F32 = jnp.float32
_BF16 = jnp.bfloat16

_HEAD_DIM = 128
_N_HEADS = 16
_N_GROUPS = 16
_ATTN_W = _N_HEADS * _HEAD_DIM
_GMLP_W = _N_GROUPS * _HEAD_DIM
_CHUNK = 128
_GRID_W = 64
_WIN_H = 8
_WIN_W = 16
_N_EXPERTS = 64
_TOP_K = 8
_N_EGROUPS = 8
_EGROUP = _N_EXPERTS // _N_EGROUPS
_TOPK_GROUPS = 4
_EXPERT_FF = 128
_ROUTED_SCALE = 2.5
_LN_EPS = 1e-5
_NEG_INF = -1e30
_LANES = 128

_V7X_VMEM_LIMIT = 56 * 1024 * 1024


def _params(n_axes):
    return pltpu.CompilerParams(dimension_semantics=("arbitrary",) * n_axes,
                                vmem_limit_bytes=_V7X_VMEM_LIMIT)


def _mm_kernel(x_ref, w_ref, o_ref):
    o_ref[...] = jnp.dot(x_ref[...], w_ref[...], preferred_element_type=_F32).astype(o_ref.dtype)


def _matmul(x, w, out_dtype, tm, tn, name):
    m, k = x.shape
    n = w.shape[1]
    tm, tn = min(tm, m), min(tn, n)
    return pl.pallas_call(
        _mm_kernel,
        out_shape=jax.ShapeDtypeStruct((m, n), out_dtype),
        grid=(m // tm, n // tn),
        in_specs=[pl.BlockSpec((tm, k), lambda i, j: (i, 0)),
                  pl.BlockSpec((k, tn), lambda i, j: (0, j))],
        out_specs=pl.BlockSpec((tm, tn), lambda i, j: (i, j)),
        compiler_params=_params(2),
        name=name,
    )(x, w)


def _mm2_kernel(a_ref, g_ref, wa_ref, wg_ref, o_ref):
    acc = jnp.dot(a_ref[...], wa_ref[...], preferred_element_type=_F32)
    acc += jnp.dot(g_ref[...], wg_ref[...], preferred_element_type=_F32)
    o_ref[...] = acc


def _out_proj(a, g, w_out, tm, tn):
    m, ka = a.shape
    kg = g.shape[1]
    n = w_out.shape[1]
    tm, tn = min(tm, m), min(tn, n)
    return pl.pallas_call(
        _mm2_kernel,
        out_shape=jax.ShapeDtypeStruct((m, n), _F32),
        grid=(m // tm, n // tn),
        in_specs=[pl.BlockSpec((tm, ka), lambda i, j: (i, 0)),
                  pl.BlockSpec((tm, kg), lambda i, j: (i, 0)),
                  pl.BlockSpec((ka, tn), lambda i, j: (0, j)),
                  pl.BlockSpec((kg, tn), lambda i, j: (1, j))],
        out_specs=pl.BlockSpec((tm, tn), lambda i, j: (i, j)),
        compiler_params=_params(2),
        name="out_proj",
    )(a, g, w_out, w_out)


def _attn_kernel(rs_ref, d0_ref, q_ref, *rest):
    del rs_ref, d0_ref
    k_refs = rest[:_WIN_H]
    v_refs = rest[_WIN_H:2 * _WIN_H]
    bias_ref, gain_ref, o_ref, acc_ref = rest[2 * _WIN_H:]
    scale = _HEAD_DIM ** -0.5
    ss = jnp.zeros((_GRID_W, 1), _F32)
    for h in range(_N_HEADS):
        sl = slice(h * _HEAD_DIM, (h + 1) * _HEAD_DIM)
        q = q_ref[:, sl]
        k = jnp.concatenate([r[:, sl] for r in k_refs], axis=0)
        v = jnp.concatenate([r[:, sl] for r in v_refs], axis=0)
        s = lax.dot_general(q, k, (((1,), (1,)), ((), ())), preferred_element_type=_F32)
        s = s * scale + bias_ref[h]
        m = jnp.max(s, axis=-1, keepdims=True)
        p = jnp.exp(s - m)
        l = jnp.sum(p, axis=-1, keepdims=True)
        o = jnp.dot(p.astype(_BF16), v, preferred_element_type=_F32) / l
        acc_ref[:, sl] = o
        ss = ss + jnp.sum(o * o, axis=-1, keepdims=True)
    r = lax.rsqrt(ss / _ATTN_W + _LN_EPS)
    o_ref[...] = (acc_ref[...] * r * gain_ref[...]).astype(o_ref.dtype)


def _attention(proj, bias_tbl, gain_a, rs_tbl, d0_tbl):
    n = proj.shape[0]
    nr = n // _GRID_W
    proj3 = proj.reshape(nr, _GRID_W, proj.shape[1])
    blk = (None, _GRID_W, _ATTN_W)
    in_specs = [pl.BlockSpec(blk, lambda r, rs, d0: (r, 0, 0))]
    for c in (1, 2):
        for i in range(_WIN_H):
            in_specs.append(pl.BlockSpec(blk, lambda r, rs, d0, i=i, c=c: (rs[r] + i, 0, c)))
    in_specs.append(pl.BlockSpec((None, _N_HEADS, _GRID_W, _WIN_H * _GRID_W),
                                 lambda r, rs, d0: (d0[r], 0, 0, 0)))
    in_specs.append(pl.BlockSpec((1, _ATTN_W), lambda r, rs, d0: (0, 0)))
    out = pl.pallas_call(
        _attn_kernel,
        out_shape=jax.ShapeDtypeStruct((nr, _GRID_W, _ATTN_W), _BF16),
        grid_spec=pltpu.PrefetchScalarGridSpec(
            num_scalar_prefetch=2,
            grid=(nr,),
            in_specs=in_specs,
            out_specs=pl.BlockSpec(blk, lambda r, rs, d0: (r, 0, 0)),
            scratch_shapes=[pltpu.VMEM((_GRID_W, _ATTN_W), _F32)],
        ),
        compiler_params=_params(1),
        name="na2d_attention",
    )(rs_tbl, d0_tbl, *([proj3] * (1 + 2 * _WIN_H)), bias_tbl, gain_a)
    return out.reshape(n, _ATTN_W)


def _attn_tables(image_rows):
    rs_tbl, d0_tbl, base = [], [], 0
    for rows in image_rows:
        kh = min(_WIN_H, rows)
        assert kh == _WIN_H
        for r in range(rows):
            rs = min(max(r - kh // 2, 0), rows - kh)
            rs_tbl.append(base + rs)
            d0_tbl.append(rs - r + _WIN_H - 1)
        base += rows
    return np.asarray(rs_tbl, np.int32), np.asarray(d0_tbl, np.int32)


def _attn_bias_table(rpb):
    cols = np.arange(_GRID_W)
    col_start = np.clip(cols - _WIN_W // 2, 0, _GRID_W - _WIN_W)
    col_mask = (cols[None, :] >= col_start[:, None]) & (cols[None, :] < col_start[:, None] + _WIN_W)
    col_off = np.clip(cols[None, :] - cols[:, None] + (_WIN_W - 1), 0, 2 * _WIN_W - 2)
    rpb_c = jnp.where(col_mask[None, None], rpb[:, :, col_off].astype(_F32), _F32(_NEG_INF))
    tabs = []
    for d0 in range(_WIN_H):
        t = rpb_c[:, d0:d0 + _WIN_H].transpose(0, 2, 1, 3)
        tabs.append(t.reshape(_N_HEADS, _GRID_W, _WIN_H * _GRID_W))
    return jnp.stack(tabs)


def _gmlp_kernel(u_ref, v_ref, ws_ref, lng_ref, lnb_ref, bsx_ref, gain_ref, o_ref, acc_ref):
    ss = jnp.zeros((_CHUNK, 1), _F32)
    for g in range(_N_GROUPS):
        sl = slice(g * _HEAD_DIM, (g + 1) * _HEAD_DIM)
        v = jax.nn.gelu(v_ref[:, sl].astype(_F32))
        mu = jnp.mean(v, axis=-1, keepdims=True)
        c = v - mu
        var = jnp.mean(c * c, axis=-1, keepdims=True)
        vn = c * lax.rsqrt(var + _LN_EPS) * lng_ref[:, sl] + lnb_ref[:, sl]
        mixed = jnp.dot(ws_ref[g], vn.astype(_BF16), preferred_element_type=_F32) + bsx_ref[:, sl]
        o = jax.nn.gelu(u_ref[:, sl].astype(_F32)) * mixed
        acc_ref[:, sl] = o
        ss = ss + jnp.sum(o * o, axis=-1, keepdims=True)
    r = lax.rsqrt(ss / _GMLP_W + _LN_EPS)
    o_ref[...] = (acc_ref[...] * r * gain_ref[...]).astype(o_ref.dtype)


def _gmlp(proj, ws, ln_g, ln_b, bsx, gain_g):
    n = proj.shape[0]
    row = pl.BlockSpec((1, _GMLP_W), lambda i: (0, 0))
    return pl.pallas_call(
        _gmlp_kernel,
        out_shape=jax.ShapeDtypeStruct((n, _GMLP_W), _BF16),
        grid=(n // _CHUNK,),
        in_specs=[pl.BlockSpec((_CHUNK, _GMLP_W), lambda i: (i, 3)),
                  pl.BlockSpec((_CHUNK, _GMLP_W), lambda i: (i, 4)),
                  pl.BlockSpec((_N_GROUPS, _CHUNK, _CHUNK), lambda i: (0, 0, 0)),
                  row, row,
                  pl.BlockSpec((_CHUNK, _GMLP_W), lambda i: (0, 0)),
                  row],
        out_specs=pl.BlockSpec((_CHUNK, _GMLP_W), lambda i: (i, 0)),
        scratch_shapes=[pltpu.VMEM((_CHUNK, _GMLP_W), _F32)],
        compiler_params=_params(1),
        name="gmlp_gating",
    )(proj, proj, ws, ln_g, ln_b, bsx, gain_g)


def _add_ln_kernel(alpha, res_ref, y_ref, g_ref, b_ref, o32_ref, o16_ref):
    z = alpha * res_ref[...] + y_ref[...]
    mu = jnp.mean(z, axis=-1, keepdims=True)
    c = z - mu
    var = jnp.mean(c * c, axis=-1, keepdims=True)
    o = c * lax.rsqrt(var + _LN_EPS) * g_ref[...] + b_ref[...]
    o32_ref[...] = o
    o16_ref[...] = o.astype(_BF16)


def _add_ln(res, y, g, b, alpha, tm):
    m, d = res.shape
    tm = min(tm, m)
    tile = pl.BlockSpec((tm, d), lambda i: (i, 0))
    row = pl.BlockSpec((1, d), lambda i: (0, 0))
    return pl.pallas_call(
        functools.partial(_add_ln_kernel, alpha),
        out_shape=(jax.ShapeDtypeStruct((m, d), _F32), jax.ShapeDtypeStruct((m, d), _BF16)),
        grid=(m // tm,),
        in_specs=[tile, tile, row, row],
        out_specs=(tile, tile),
        compiler_params=_params(1),
        name="residual_layernorm",
    )(res, y, g, b)


def _first_argmax(x, iota, n):
    m = jnp.max(x, axis=0, keepdims=True)
    idx = jnp.min(jnp.where(x == m, iota, n), axis=0, keepdims=True)
    return m, iota == idx


def _router_kernel(h_ref, wr_ref, bias_ref, g_ref):
    tm = h_ref.shape[0]
    logits = jnp.dot(h_ref[...], wr_ref[...], precision=lax.Precision.HIGHEST,
                     preferred_element_type=_F32)
    scores = jax.nn.sigmoid(logits.T[:_N_EXPERTS])
    bias = jnp.concatenate([bias_ref[...]] * (tm // _LANES), axis=1)
    sel = scores + bias
    ninf = _F32(-jnp.inf)

    iota_g = lax.broadcasted_iota(jnp.int32, (_EGROUP, tm), 0)
    grp = []
    for g in range(_N_EGROUPS):
        x = sel[g * _EGROUP:(g + 1) * _EGROUP]
        m1, hit = _first_argmax(x, iota_g, _EGROUP)
        m2 = jnp.max(jnp.where(hit, ninf, x), axis=0, keepdims=True)
        grp.append(m1 + m2)
    grp = jnp.concatenate(grp, axis=0)

    iota_n = lax.broadcasted_iota(jnp.int32, (_N_EGROUPS, tm), 0)
    gmask = jnp.zeros((_N_EGROUPS, tm), _F32)
    for _ in range(_TOPK_GROUPS):
        _, hit = _first_argmax(grp, iota_n, _N_EGROUPS)
        gmask = jnp.where(hit, 1.0, gmask)
        grp = jnp.where(hit, ninf, grp)

    masked = jnp.concatenate(
        [jnp.where(gmask[g:g + 1] > 0, sel[g * _EGROUP:(g + 1) * _EGROUP], ninf) for g in range(_N_EGROUPS)], axis=0)
    iota_e = lax.broadcasted_iota(jnp.int32, (_N_EXPERTS, tm), 0)
    chosen = jnp.zeros((_N_EXPERTS, tm), _F32)
    for _ in range(_TOP_K):
        _, hit = _first_argmax(masked, iota_e, _N_EXPERTS)
        chosen = jnp.where(hit, 1.0, chosen)
        masked = jnp.where(hit, ninf, masked)

    w = scores * chosen
    gates = w / jnp.sum(w, axis=0, keepdims=True) * _ROUTED_SCALE
    pad = (lax.broadcasted_iota(jnp.int32, (_LANES - _N_EXPERTS, tm), 0) == 0).astype(_F32)
    g_ref[...] = jnp.concatenate([gates, pad], axis=0).T


def _router(h, w_router_p, bias_p, tm):
    m, d = h.shape
    tm = min(tm, m)
    return pl.pallas_call(
        _router_kernel,
        out_shape=jax.ShapeDtypeStruct((m, _LANES), _F32),
        grid=(m // tm,),
        in_specs=[pl.BlockSpec((tm, d), lambda i: (i, 0)),
                  pl.BlockSpec((d, _LANES), lambda i: (0, 0)),
                  pl.BlockSpec((_N_EXPERTS, _LANES), lambda i: (0, 0))],
        out_specs=pl.BlockSpec((tm, _LANES), lambda i: (i, 0)),
        compiler_params=_params(1),
        name="moe_router",
    )(h, w_router_p, bias_p)


def _moe_kernel(x_ref, gates_ref, e_ref, wg_ref, wu_ref, wd_ref, o_ref):
    j = pl.program_id(1)
    x = x_ref[...]
    hg = jnp.dot(x, wg_ref[...], preferred_element_type=_F32)
    hu = jnp.dot(x, wu_ref[...], preferred_element_type=_F32)
    gexp = jnp.dot(gates_ref[...], e_ref[...], precision=lax.Precision.HIGHEST,
                   preferred_element_type=_F32)
    hid = (jax.nn.silu(hg) * hu * gexp).astype(_BF16)
    contrib = jnp.dot(hid, wd_ref[...], preferred_element_type=_F32)

    @pl.when(j == 0)
    def _():
        o_ref[...] = contrib

    @pl.when(j > 0)
    def _():
        o_ref[...] += contrib


def _moe(hb, gates, expand, wg_all, wu_all, wd_all, tm, tf):
    m, d = hb.shape
    ff = wg_all.shape[1]
    tm = min(tm, m)
    return pl.pallas_call(
        _moe_kernel,
        out_shape=jax.ShapeDtypeStruct((m, d), _F32),
        grid=(m // tm, ff // tf),
        in_specs=[pl.BlockSpec((tm, d), lambda i, j: (i, 0)),
                  pl.BlockSpec((tm, _LANES), lambda i, j: (i, 0)),
                  pl.BlockSpec((None, _LANES, tf), lambda i, j: (j, 0, 0)),
                  pl.BlockSpec((d, tf), lambda i, j: (0, j)),
                  pl.BlockSpec((d, tf), lambda i, j: (0, j)),
                  pl.BlockSpec((tf, d), lambda i, j: (j, 0))],
        out_specs=pl.BlockSpec((tm, d), lambda i, j: (i, 0)),
        compiler_params=_params(2),
        name="moe_ffn",
    )(hb, gates, expand, wg_all, wu_all, wd_all)


def _gate_expansion(ff_routed, ff_total, tf):
    col = np.arange(ff_total)
    lane = np.where(col < ff_routed, col // _EXPERT_FF, _N_EXPERTS)
    e = (np.arange(_LANES)[:, None] == lane[None, :]).astype(np.float32)
    return e.reshape(_LANES, ff_total // tf, tf).transpose(1, 0, 2)


def _ple_kernel(hb_ref, h_ref, p_ref, wpg_ref, wple_ref, o32_ref, o16_ref):
    gate = jax.nn.sigmoid(jnp.dot(hb_ref[...], wpg_ref[...], preferred_element_type=_F32))
    pe = jnp.dot(p_ref[...].astype(_BF16), wple_ref[...], preferred_element_type=_F32)
    o = h_ref[...] + gate * pe
    o32_ref[...] = o
    o16_ref[...] = o.astype(_BF16)


def _ple(hb, h, p, wpg, wple, tm, tn):
    m, d = hb.shape
    pd = p.shape[1]
    tm, tn = min(tm, m), min(tn, d)
    out_tile = pl.BlockSpec((tm, tn), lambda i, j: (i, j))
    return pl.pallas_call(
        _ple_kernel,
        out_shape=(jax.ShapeDtypeStruct((m, d), _F32), jax.ShapeDtypeStruct((m, d), _BF16)),
        grid=(m // tm, d // tn),
        in_specs=[pl.BlockSpec((tm, d), lambda i, j: (i, 0)),
                  out_tile,
                  pl.BlockSpec((tm, pd), lambda i, j: (i, 0)),
                  pl.BlockSpec((d, tn), lambda i, j: (0, j)),
                  pl.BlockSpec((pd, tn), lambda i, j: (0, j))],
        out_specs=(out_tile, out_tile),
        compiler_params=_params(2),
        name="ple_gate",
    )(hb, h, p, wpg, wple)


def _layer(x, xb, p, image_rows, alpha, w):
    (w_in, attn_rpb, gmlp_ln_g, gmlp_ln_b, gmlp_ws, gmlp_bs, mix_norm_g, w_out, ln1_g, ln1_b,
     w_router, router_bias, w_e_gate, w_e_up, w_e_down, w_sh_gate, w_sh_up, w_sh_down,
     ln2_g, ln2_b, w_ple, w_ple_gate) = w
    d = x.shape[1]
    row = lambda v: v.reshape(1, -1).astype(_F32)

    proj = _matmul(xb, w_in.astype(_BF16), _BF16, 1024, 1024, "in_proj")

    rs_tbl, d0_tbl = _attn_tables(image_rows)
    a = _attention(proj, _attn_bias_table(attn_rpb), row(mix_norm_g[:_ATTN_W]),
                   jnp.asarray(rs_tbl), jnp.asarray(d0_tbl))

    bsx = jnp.repeat(gmlp_bs.T.astype(_F32), _HEAD_DIM, axis=1)
    g = _gmlp(proj, gmlp_ws.astype(_BF16), row(gmlp_ln_g), row(gmlp_ln_b), bsx, row(mix_norm_g[_ATTN_W:]))

    mixed_out = _out_proj(a, g, w_out.astype(_BF16), 1024, 1024)
    h, hb = _add_ln(x, mixed_out, row(ln1_g), row(ln1_b), alpha, 256)

    wr_p = jnp.pad(w_router.astype(_F32), ((0, 0), (0, _LANES - _N_EXPERTS)))
    bias_p = jnp.broadcast_to(router_bias.astype(_F32)[:, None], (_N_EXPERTS, _LANES))
    gates = _router(h, wr_p, bias_p, 512)

    ff_routed = _N_EXPERTS * _EXPERT_FF
    cat_e = lambda we, ws: jnp.concatenate(
        [we.astype(_BF16).transpose(1, 0, 2).reshape(d, ff_routed), ws.astype(_BF16)], axis=1)
    wg_all = cat_e(w_e_gate, w_sh_gate)
    wu_all = cat_e(w_e_up, w_sh_up)
    wd_all = jnp.concatenate([w_e_down.astype(_BF16).reshape(ff_routed, d), w_sh_down.astype(_BF16)], axis=0)
    tf = 256
    expand = jnp.asarray(_gate_expansion(ff_routed, wg_all.shape[1], tf))
    moe_out = _moe(hb, gates, expand, wg_all, wu_all, wd_all, 512, tf)

    h2, h2b = _add_ln(h, moe_out, row(ln2_g), row(ln2_b), alpha, 256)
    return _ple(h2b, h2, p, w_ple_gate.astype(_BF16), w_ple.astype(_BF16), 1024, 512)


def kernel(x_prompt, x_sample, p_prompt, p_sample, w_in, attn_rpb, gmlp_ln_g, gmlp_ln_b, gmlp_ws, gmlp_bs,
           mix_norm_g, w_out, ln1_g, ln1_b, w_router, router_bias, w_e_gate, w_e_up, w_e_down,
           w_sh_gate, w_sh_up, w_sh_down, ln2_g, ln2_b, w_ple, w_ple_gate):
    depth = w_in.shape[0]
    d = x_prompt.shape[-1]
    alpha = (2 * depth) ** 0.25
    groups = (x_prompt, x_sample)
    image_rows = [xg.shape[1] // _GRID_W for xg in groups for _ in range(xg.shape[0])]
    x = jnp.concatenate([xg.reshape(-1, d) for xg in groups], axis=0)
    xb = x.astype(_BF16)
    weights = (w_in, attn_rpb, gmlp_ln_g, gmlp_ln_b, gmlp_ws, gmlp_bs, mix_norm_g, w_out, ln1_g, ln1_b,
               w_router, router_bias, w_e_gate, w_e_up, w_e_down, w_sh_gate, w_sh_up, w_sh_down,
               ln2_g, ln2_b, w_ple, w_ple_gate)
    for i in range(depth):
        p = jnp.concatenate([pg[i].reshape(-1, pg.shape[-1]) for pg in (p_prompt, p_sample)], axis=0)
        x, xb = _layer(x, xb, p, image_rows, alpha, tuple(wt[i] for wt in weights))
    n_prompt = x_prompt.shape[0] * x_prompt.shape[1]
    return x[:n_prompt].reshape(x_prompt.shape), x[n_prompt:].reshape(x_sample.shape)
```

```python
import functools

import numpy as np
import jax
import jax.numpy as jnp
from jax import lax
from jax.experimental import pallas as pl
from jax.experimental.pallas import tpu as pltpu

_F32 = jnp.float32
_BF16 = jnp.bfloat16
_U32 = jnp.uint32

_HEAD_DIM = 128
_N_HEADS = 16
_N_GROUPS = 16
_ATTN_W = _N_HEADS * _HEAD_DIM
_GMLP_W = _N_GROUPS * _HEAD_DIM
_CHUNK = 128
_GRID_W = 64
_WIN_H = 8
_WIN_W = 16
_N_EXPERTS = 64
_TOP_K = 8
_N_EGROUPS = 8
_EGROUP = _N_EXPERTS // _N_EGROUPS
_TOPK_GROUPS = 4
_EXPERT_FF = 128
_ROUTED_SCALE = 2.5
_LN_EPS = 1e-5
_NEG_INF = -1e30
_LANES = 128
_EXPERT_TILE = 256
_GATHER_TOKENS = 128
_SLAB = 16

_V7X_VMEM_LIMIT = 56 * 1024 * 1024


def _params(n_axes):
    return pltpu.CompilerParams(dimension_semantics=("arbitrary",) * n_axes,
                                vmem_limit_bytes=_V7X_VMEM_LIMIT)


def _mm_kernel(x_ref, w_ref, o_ref):
    o_ref[...] = jnp.dot(x_ref[...], w_ref[...], preferred_element_type=_F32).astype(o_ref.dtype)


def _matmul(x, w, out_dtype, tm, tn, name):
    m, k = x.shape
    n = w.shape[1]
    tm, tn = min(tm, m), min(tn, n)
    return pl.pallas_call(
        _mm_kernel,
        out_shape=jax.ShapeDtypeStruct((m, n), out_dtype),
        grid=(m // tm, n // tn),
        in_specs=[pl.BlockSpec((tm, k), lambda i, j: (i, 0)),
                  pl.BlockSpec((k, tn), lambda i, j: (0, j))],
        out_specs=pl.BlockSpec((tm, tn), lambda i, j: (i, j)),
        compiler_params=_params(2),
        name=name,
    )(x, w)


def _mm2_kernel(a_ref, g_ref, wa_ref, wg_ref, o_ref):
    acc = jnp.dot(a_ref[...], wa_ref[...], preferred_element_type=_F32)
    acc += jnp.dot(g_ref[...], wg_ref[...], preferred_element_type=_F32)
    o_ref[...] = acc


def _out_proj(a, g, w_out, tm, tn):
    m, ka = a.shape
    kg = g.shape[1]
    n = w_out.shape[1]
    tm, tn = min(tm, m), min(tn, n)
    return pl.pallas_call(
        _mm2_kernel,
        out_shape=jax.ShapeDtypeStruct((m, n), _F32),
        grid=(m // tm, n // tn),
        in_specs=[pl.BlockSpec((tm, ka), lambda i, j: (i, 0)),
                  pl.BlockSpec((tm, kg), lambda i, j: (i, 0)),
                  pl.BlockSpec((ka, tn), lambda i, j: (0, j)),
                  pl.BlockSpec((kg, tn), lambda i, j: (1, j))],
        out_specs=pl.BlockSpec((tm, tn), lambda i, j: (i, j)),
        compiler_params=_params(2),
        name="out_proj",
    )(a, g, w_out, w_out)


def _attn_kernel(rs_ref, d0_ref, q_ref, *rest):
    del rs_ref, d0_ref
    k_refs = rest[:_WIN_H]
    v_refs = rest[_WIN_H:2 * _WIN_H]
    bias_ref, gain_ref, o_ref, acc_ref = rest[2 * _WIN_H:]
    scale = _HEAD_DIM ** -0.5
    ss = jnp.zeros((_GRID_W, 1), _F32)
    for h in range(_N_HEADS):
        sl = slice(h * _HEAD_DIM, (h + 1) * _HEAD_DIM)
        q = q_ref[:, sl]
        k = jnp.concatenate([r[:, sl] for r in k_refs], axis=0)
        v = jnp.concatenate([r[:, sl] for r in v_refs], axis=0)
        s = lax.dot_general(q, k, (((1,), (1,)), ((), ())), preferred_element_type=_F32)
        s = s * scale + bias_ref[h]
        m = jnp.max(s, axis=-1, keepdims=True)
        p = jnp.exp(s - m)
        l = jnp.sum(p, axis=-1, keepdims=True)
        o = jnp.dot(p.astype(_BF16), v, preferred_element_type=_F32) / l
        acc_ref[:, sl] = o
        ss = ss + jnp.sum(o * o, axis=-1, keepdims=True)
    r = lax.rsqrt(ss / _ATTN_W + _LN_EPS)
    o_ref[...] = (acc_ref[...] * r * gain_ref[...]).astype(o_ref.dtype)


def _attention(proj, bias_tbl, gain_a, rs_tbl, d0_tbl):
    n = proj.shape[0]
    nr = n // _GRID_W
    proj3 = proj.reshape(nr, _GRID_W, proj.shape[1])
    blk = (None, _GRID_W, _ATTN_W)
    in_specs = [pl.BlockSpec(blk, lambda r, rs, d0: (r, 0, 0))]
    for c in (1, 2):
        for i in range(_WIN_H):
            in_specs.append(pl.BlockSpec(blk, lambda r, rs, d0, i=i, c=c: (rs[r] + i, 0, c)))
    in_specs.append(pl.BlockSpec((None, _N_HEADS, _GRID_W, _WIN_H * _GRID_W),
                                 lambda r, rs, d0: (d0[r], 0, 0, 0)))
    in_specs.append(pl.BlockSpec((1, _ATTN_W), lambda r, rs, d0: (0, 0)))
    out = pl.pallas_call(
        _attn_kernel,
        out_shape=jax.ShapeDtypeStruct((nr, _GRID_W, _ATTN_W), _BF16),
        grid_spec=pltpu.PrefetchScalarGridSpec(
            num_scalar_prefetch=2,
            grid=(nr,),
            in_specs=in_specs,
            out_specs=pl.BlockSpec(blk, lambda r, rs, d0: (r, 0, 0)),
            scratch_shapes=[pltpu.VMEM((_GRID_W, _ATTN_W), _F32)],
        ),
        compiler_params=_params(1),
        name="na2d_attention",
    )(rs_tbl, d0_tbl, *([proj3] * (1 + 2 * _WIN_H)), bias_tbl, gain_a)
    return out.reshape(n, _ATTN_W)


def _attn_tables(image_rows):
    rs_tbl, d0_tbl, base = [], [], 0
    for rows in image_rows:
        kh = min(_WIN_H, rows)
        assert kh == _WIN_H
        for r in range(rows):
            rs = min(max(r - kh // 2, 0), rows - kh)
            rs_tbl.append(base + rs)
            d0_tbl.append(rs - r + _WIN_H - 1)
        base += rows
    return np.asarray(rs_tbl, np.int32), np.asarray(d0_tbl, np.int32)


def _attn_bias_table(rpb):
    cols = np.arange(_GRID_W)
    col_start = np.clip(cols - _WIN_W // 2, 0, _GRID_W - _WIN_W)
    col_mask = (cols[None, :] >= col_start[:, None]) & (cols[None, :] < col_start[:, None] + _WIN_W)
    col_off = np.clip(cols[None, :] - cols[:, None] + (_WIN_W - 1), 0, 2 * _WIN_W - 2)
    rpb_c = jnp.where(col_mask[None, None], rpb[:, :, col_off].astype(_F32), _F32(_NEG_INF))
    tabs = []
    for d0 in range(_WIN_H):
        t = rpb_c[:, d0:d0 + _WIN_H].transpose(0, 2, 1, 3)
        tabs.append(t.reshape(_N_HEADS, _GRID_W, _WIN_H * _GRID_W))
    return jnp.stack(tabs)


def _gmlp_kernel(u_ref, v_ref, ws_ref, lng_ref, lnb_ref, bsx_ref, gain_ref, o_ref, acc_ref):
    ss = jnp.zeros((_CHUNK, 1), _F32)
    for g in range(_N_GROUPS):
        sl = slice(g * _HEAD_DIM, (g + 1) * _HEAD_DIM)
        v = jax.nn.gelu(v_ref[:, sl].astype(_F32))
        mu = jnp.mean(v, axis=-1, keepdims=True)
        c = v - mu
        var = jnp.mean(c * c, axis=-1, keepdims=True)
        vn = c * lax.rsqrt(var + _LN_EPS) * lng_ref[:, sl] + lnb_ref[:, sl]
        mixed = jnp.dot(ws_ref[g], vn.astype(_BF16), preferred_element_type=_F32) + bsx_ref[:, sl]
        o = jax.nn.gelu(u_ref[:, sl].astype(_F32)) * mixed
        acc_ref[:, sl] = o
        ss = ss + jnp.sum(o * o, axis=-1, keepdims=True)
    r = lax.rsqrt(ss / _GMLP_W + _LN_EPS)
    o_ref[...] = (acc_ref[...] * r * gain_ref[...]).astype(o_ref.dtype)


def _gmlp(proj, ws, ln_g, ln_b, bsx, gain_g):
    n = proj.shape[0]
    row = pl.BlockSpec((1, _GMLP_W), lambda i: (0, 0))
    return pl.pallas_call(
        _gmlp_kernel,
        out_shape=jax.ShapeDtypeStruct((n, _GMLP_W), _BF16),
        grid=(n // _CHUNK,),
        in_specs=[pl.BlockSpec((_CHUNK, _GMLP_W), lambda i: (i, 3)),
                  pl.BlockSpec((_CHUNK, _GMLP_W), lambda i: (i, 4)),
                  pl.BlockSpec((_N_GROUPS, _CHUNK, _CHUNK), lambda i: (0, 0, 0)),
                  row, row,
                  pl.BlockSpec((_CHUNK, _GMLP_W), lambda i: (0, 0)),
                  row],
        out_specs=pl.BlockSpec((_CHUNK, _GMLP_W), lambda i: (i, 0)),
        scratch_shapes=[pltpu.VMEM((_CHUNK, _GMLP_W), _F32)],
        compiler_params=_params(1),
        name="gmlp_gating",
    )(proj, proj, ws, ln_g, ln_b, bsx, gain_g)


def _layer_norm(z, g, b):
    mu = jnp.mean(z, axis=-1, keepdims=True)
    c = z - mu
    var = jnp.mean(c * c, axis=-1, keepdims=True)
    return c * lax.rsqrt(var + _LN_EPS) * g + b


def _pack_halves(x):
    k = x.shape[1] // 2
    bits = lambda v: lax.bitcast_convert_type(v.astype(_BF16).astype(_F32), _U32)
    return (bits(x[:, k:]) & _U32(0xFFFF0000)) | (bits(x[:, :k]) >> _U32(16))


def _unpack_halves(words):
    lo = lax.bitcast_convert_type(words << _U32(16), _F32)
    hi = lax.bitcast_convert_type(words & _U32(0xFFFF0000), _F32)
    return lo, hi


def _store_slabs(ref, first, words):
    m = words.shape[0]
    for s in range(_SLAB):
        ref[pl.ds(first * _SLAB + s, m, stride=_SLAB), :] = words[:, s * _LANES:(s + 1) * _LANES]


def _load_slabs(ref, first, m):
    return jnp.concatenate([ref[pl.ds(first * _SLAB + s, m, stride=_SLAB), :] for s in range(_SLAB)], axis=1)


def _slab_copy(src_ref, src_slab, dst_ref, dst_slab, sem):
    src = src_ref.at[pl.ds(pl.multiple_of(src_slab * _SLAB, _SLAB), _SLAB)]
    dst = dst_ref.at[pl.ds(pl.multiple_of(dst_slab * _SLAB, _SLAB), _SLAB)]
    return pltpu.make_async_copy(src, dst, sem)


def _add_ln_kernel(alpha, res_ref, y_ref, g_ref, b_ref, o32_ref, o16_ref, opk_ref):
    o = _layer_norm(alpha * res_ref[...] + y_ref[...], g_ref[...], b_ref[...])
    o32_ref[...] = o
    o16_ref[...] = o.astype(_BF16)
    _store_slabs(opk_ref, 0, _pack_halves(o))


def _add_ln(res, y, g, b, alpha, tm):
    m, d = res.shape
    tm = min(tm, m)
    tile = pl.BlockSpec((tm, d), lambda i: (i, 0))
    half = pl.BlockSpec((tm * _SLAB, _LANES), lambda i: (i, 0))
    row = pl.BlockSpec((1, d), lambda i: (0, 0))
    return pl.pallas_call(
        functools.partial(_add_ln_kernel, alpha),
        out_shape=(jax.ShapeDtypeStruct((m, d), _F32), jax.ShapeDtypeStruct((m, d), _BF16),
                   jax.ShapeDtypeStruct((m * _SLAB, _LANES), _U32)),
        grid=(m // tm,),
        in_specs=[tile, tile, row, row],
        out_specs=(tile, tile, half),
        compiler_params=_params(1),
        name="residual_layernorm",
    )(res, y, g, b)


def _first_argmax(x, iota, n):
    m = jnp.max(x, axis=0, keepdims=True)
    return jnp.min(jnp.where(x == m, iota, n), axis=0, keepdims=True)


def _router_kernel(h_ref, wr_ref, bias_ref, tri_ref, e_ref, w_ref, r_ref, cnt_ref, carry_ref):
    tm = h_ref.shape[0]

    @pl.when(pl.program_id(0) == 0)
    def _():
        carry_ref[...] = jnp.zeros_like(carry_ref)

    logits = jnp.dot(h_ref[...], wr_ref[...], precision=lax.Precision.HIGHEST,
                     preferred_element_type=_F32)
    scores = jax.nn.sigmoid(logits.T[:_N_EXPERTS])
    bias = jnp.concatenate([bias_ref[...]] * (tm // _LANES), axis=1)
    sel = scores + bias
    ninf = _F32(-jnp.inf)

    iota_g = lax.broadcasted_iota(jnp.int32, (_EGROUP, tm), 0)
    grp = []
    for g in range(_N_EGROUPS):
        x = sel[g * _EGROUP:(g + 1) * _EGROUP]
        m1 = jnp.max(x, axis=0, keepdims=True)
        hit = iota_g == _first_argmax(x, iota_g, _EGROUP)
        m2 = jnp.max(jnp.where(hit, ninf, x), axis=0, keepdims=True)
        grp.append(m1 + m2)
    grp = jnp.concatenate(grp, axis=0)

    iota_n = lax.broadcasted_iota(jnp.int32, (_N_EGROUPS, tm), 0)
    gmask = jnp.zeros((_N_EGROUPS, tm), _F32)
    for _ in range(_TOPK_GROUPS):
        hit = iota_n == _first_argmax(grp, iota_n, _N_EGROUPS)
        gmask = jnp.where(hit, 1.0, gmask)
        grp = jnp.where(hit, ninf, grp)

    masked = jnp.concatenate(
        [jnp.where(gmask[g:g + 1] > 0, sel[g * _EGROUP:(g + 1) * _EGROUP], ninf) for g in range(_N_EGROUPS)], axis=0)
    iota_e = lax.broadcasted_iota(jnp.int32, (_N_EXPERTS, tm), 0)
    chosen = jnp.zeros((_N_EXPERTS, tm), _F32)
    picks = []
    for _ in range(_TOP_K):
        idx = _first_argmax(masked, iota_e, _N_EXPERTS)
        hit = iota_e == idx
        chosen = jnp.where(hit, 1.0, chosen)
        masked = jnp.where(hit, ninf, masked)
        picks.append(idx)

    w = scores * chosen
    gates = w / jnp.sum(w, axis=0, keepdims=True) * _ROUTED_SCALE

    carry = carry_ref[...]
    rank = carry[:, 0:1] + jnp.dot(chosen.astype(_BF16), tri_ref[...], preferred_element_type=_F32)
    carry = carry + jnp.sum(chosen, axis=1, keepdims=True)
    carry_ref[...] = carry
    cnt_ref[...] = carry

    pick_w, pick_r = [], []
    for idx in picks:
        hit = iota_e == idx
        pick_w.append(jnp.sum(jnp.where(hit, gates, 0.0), axis=0, keepdims=True))
        pick_r.append(jnp.sum(jnp.where(hit, rank, 0.0), axis=0, keepdims=True))
    e_ref[...] = jnp.concatenate(picks, axis=0)
    w_ref[...] = jnp.concatenate(pick_w, axis=0)
    r_ref[...] = jnp.concatenate(pick_r, axis=0).astype(jnp.int32)


def _router(h, w_router_p, bias_p, tm):
    m, d = h.shape
    tm = min(tm, m)
    tri = jnp.asarray(np.triu(np.ones((tm, tm), np.float32), 1), _BF16)
    slot = pl.BlockSpec((_TOP_K, tm), lambda i: (0, i))
    return pl.pallas_call(
        _router_kernel,
        out_shape=(jax.ShapeDtypeStruct((_TOP_K, m), jnp.int32), jax.ShapeDtypeStruct((_TOP_K, m), _F32),
                   jax.ShapeDtypeStruct((_TOP_K, m), jnp.int32), jax.ShapeDtypeStruct((_N_EXPERTS, _LANES), _F32)),
        grid=(m // tm,),
        in_specs=[pl.BlockSpec((tm, d), lambda i: (i, 0)),
                  pl.BlockSpec((d, _LANES), lambda i: (0, 0)),
                  pl.BlockSpec((_N_EXPERTS, _LANES), lambda i: (0, 0)),
                  pl.BlockSpec((tm, tm), lambda i: (0, 0))],
        out_specs=(slot, slot, slot, pl.BlockSpec((_N_EXPERTS, _LANES), lambda i: (0, 0))),
        scratch_shapes=[pltpu.VMEM((_N_EXPERTS, _LANES), _F32)],
        compiler_params=_params(1),
        name="moe_router",
    )(h, w_router_p, bias_p, tri)


def _pad_chunks():
    return [1 << b for b in range(_EXPERT_TILE.bit_length() - 1)]


def _dispatch_kernel(pad_start_ref, pad_len_ref, pos_ref, x_ref, xs_ref, zero_ref, sem, zsem):
    tm = x_ref.shape[0] // _SLAB

    def start(t, c):
        for k in range(_TOP_K):
            _slab_copy(x_ref, t, xs_ref, pos_ref[t * _TOP_K + k], sem).start()
        return c

    def wait(t, c):
        for k in range(_TOP_K):
            _slab_copy(x_ref, 0, xs_ref, 0, sem).wait()
        return c

    lax.fori_loop(0, tm, start, 0)

    @pl.when(pl.program_id(0) == 0)
    def _():
        zero_ref[...] = jnp.zeros_like(zero_ref)

        def fill(e, do_start):
            cur = pad_start_ref[e]
            n = pad_len_ref[e]
            for c in _pad_chunks():
                @pl.when((n & c) != 0)
                def _(cur=cur, c=c):
                    dst = xs_ref.at[pl.ds(pl.multiple_of(cur * _SLAB, _SLAB), c * _SLAB)]
                    cp = pltpu.make_async_copy(zero_ref.at[pl.ds(0, c * _SLAB)], dst, zsem)
                    if do_start:
                        cp.start()
                    else:
                        cp.wait()
                cur = cur + (n & c)

        n_pads = pad_start_ref.shape[0]
        lax.fori_loop(0, n_pads, lambda e, c: (fill(e, True), c)[1], 0)
        lax.fori_loop(0, n_pads, lambda e, c: (fill(e, False), c)[1], 0)

    lax.fori_loop(0, tm, wait, 0)


def _dispatch(xpk, pos_flat, pad_start, pad_len, n_rows):
    tm = _GATHER_TOKENS
    return pl.pallas_call(
        _dispatch_kernel,
        out_shape=jax.ShapeDtypeStruct((n_rows * _SLAB, _LANES), _U32),
        grid_spec=pltpu.PrefetchScalarGridSpec(
            num_scalar_prefetch=2,
            grid=(xpk.shape[0] // (tm * _SLAB),),
            in_specs=[pl.BlockSpec((tm * _TOP_K,), lambda i, ps, pn: (i,), memory_space=pltpu.SMEM),
                      pl.BlockSpec((tm * _SLAB, _LANES), lambda i, ps, pn: (i, 0))],
            out_specs=pl.BlockSpec(memory_space=pl.ANY),
            scratch_shapes=[pltpu.VMEM((_EXPERT_TILE // 2 * _SLAB, _LANES), _U32),
                            pltpu.SemaphoreType.DMA(()), pltpu.SemaphoreType.DMA(())],
        ),
        compiler_params=_params(1),
        name="moe_dispatch",
    )(pad_start, pad_len, pos_flat, xpk)


def _experts_kernel(te_ref, nu_ref, xs_ref, wgu_ref, wd_ref, ys_ref):
    @pl.when(pl.program_id(0) >= nu_ref[0])
    def _():
        ys_ref[...] = jnp.zeros_like(ys_ref)

    @pl.when(pl.program_id(0) < nu_ref[0])
    def _():
        lo, hi = _unpack_halves(_load_slabs(xs_ref, 0, _EXPERT_TILE))
        k = lo.shape[1]
        gu = jnp.dot(lo.astype(_BF16), wgu_ref[:k], preferred_element_type=_F32)
        gu += jnp.dot(hi.astype(_BF16), wgu_ref[k:], preferred_element_type=_F32)
        act = jax.nn.silu(gu[:, :_EXPERT_FF]) * gu[:, _EXPERT_FF:]
        y = jnp.dot(act.astype(_BF16), wd_ref[...], preferred_element_type=_F32)
        _store_slabs(ys_ref, 0, _pack_halves(y))


def _experts(xs, wgu, wd, tile_expert, n_used):
    n_rows = xs.shape[0] // _SLAB
    d = wd.shape[2]
    tile = lambda t, te, nu: (t, 0)
    return pl.pallas_call(
        _experts_kernel,
        out_shape=jax.ShapeDtypeStruct(xs.shape, _U32),
        grid_spec=pltpu.PrefetchScalarGridSpec(
            num_scalar_prefetch=2,
            grid=(n_rows // _EXPERT_TILE,),
            in_specs=[pl.BlockSpec((_EXPERT_TILE * _SLAB, _LANES), tile),
                      pl.BlockSpec((None, d, 2 * _EXPERT_FF), lambda t, te, nu: (te[t], 0, 0)),
                      pl.BlockSpec((None, _EXPERT_FF, d), lambda t, te, nu: (te[t], 0, 0))],
            out_specs=pl.BlockSpec((_EXPERT_TILE * _SLAB, _LANES), tile),
        ),
        compiler_params=_params(1),
        name="moe_experts",
    )(tile_expert, n_used, xs, wgu, wd)


def _mlp_kernel(x_ref, wg_ref, wu_ref, wd_ref, o_ref):
    j = pl.program_id(1)
    x = x_ref[...]
    hg = jnp.dot(x, wg_ref[...], preferred_element_type=_F32)
    hu = jnp.dot(x, wu_ref[...], preferred_element_type=_F32)
    contrib = jnp.dot((jax.nn.silu(hg) * hu).astype(_BF16), wd_ref[...], preferred_element_type=_F32)

    @pl.when(j == 0)
    def _():
        o_ref[...] = contrib

    @pl.when(j > 0)
    def _():
        o_ref[...] += contrib


def _mlp(xb, wg, wu, wd, tm, tf):
    m, d = xb.shape
    ff = wg.shape[1]
    tm, tf = min(tm, m), min(tf, ff)
    return pl.pallas_call(
        _mlp_kernel,
        out_shape=jax.ShapeDtypeStruct((m, d), _F32),
        grid=(m // tm, ff // tf),
        in_specs=[pl.BlockSpec((tm, d), lambda i, j: (i, 0)),
                  pl.BlockSpec((d, tf), lambda i, j: (0, j)),
                  pl.BlockSpec((d, tf), lambda i, j: (0, j)),
                  pl.BlockSpec((tf, d), lambda i, j: (j, 0))],
        out_specs=pl.BlockSpec((tm, d), lambda i, j: (i, 0)),
        compiler_params=_params(2),
        name="shared_expert",
    )(xb, wg, wu, wd)


def _combine_kernel(alpha, pos_ref, w_ref, ys_ref, sh_ref, h_ref, g_ref, b_ref, o32_ref, o16_ref, buf_ref, sem):
    tm = h_ref.shape[0]

    def start(t, c):
        for k in range(_TOP_K):
            _slab_copy(ys_ref, pos_ref[t * _TOP_K + k], buf_ref, k * tm + t, sem).start()
        return c

    def wait(t, c):
        for k in range(_TOP_K):
            _slab_copy(ys_ref, 0, buf_ref, 0, sem).wait()
        return c

    lax.fori_loop(0, tm, start, 0)
    lax.fori_loop(0, tm, wait, 0)

    acc_lo = acc_hi = None
    for k in range(_TOP_K):
        lo, hi = _unpack_halves(_load_slabs(buf_ref, k * tm, tm))
        wk = w_ref[:, k:k + 1]
        acc_lo = lo * wk if acc_lo is None else acc_lo + lo * wk
        acc_hi = hi * wk if acc_hi is None else acc_hi + hi * wk
    moe = jnp.concatenate([acc_lo, acc_hi], axis=1) + sh_ref[...]
    o = _layer_norm(alpha * h_ref[...] + moe, g_ref[...], b_ref[...])
    o32_ref[...] = o
    o16_ref[...] = o.astype(_BF16)


def _combine(ys, pos_flat, w_tok, shared, h, g, b, alpha):
    m, d = h.shape
    tm = _GATHER_TOKENS
    tile = pl.BlockSpec((tm, d), lambda i: (i, 0))
    row = pl.BlockSpec((1, d), lambda i: (0, 0))
    return pl.pallas_call(
        functools.partial(_combine_kernel, alpha),
        out_shape=(jax.ShapeDtypeStruct((m, d), _F32), jax.ShapeDtypeStruct((m, d), _BF16)),
        grid=(m // tm,),
        in_specs=[pl.BlockSpec((tm * _TOP_K,), lambda i: (i,), memory_space=pltpu.SMEM),
                  pl.BlockSpec((tm, _TOP_K), lambda i: (i, 0)),
                  pl.BlockSpec(memory_space=pl.ANY),
                  tile, tile, row, row],
        out_specs=(tile, tile),
        scratch_shapes=[pltpu.VMEM((_TOP_K * tm * _SLAB, _LANES), _U32), pltpu.SemaphoreType.DMA(())],
        compiler_params=_params(1),
        name="moe_combine_layernorm",
    )(pos_flat, w_tok, ys, shared, h, g, b)


def _routing_plan(e_t, r_t, counts, n_tiles):
    counts = counts.astype(jnp.int32)
    padded = (counts + _EXPERT_TILE - 1) // _EXPERT_TILE * _EXPERT_TILE
    ends = jnp.cumsum(padded)
    offsets = ends - padded
    pos = jnp.take(offsets, e_t) + r_t
    tile_expert = jnp.searchsorted(ends, jnp.arange(n_tiles, dtype=jnp.int32) * _EXPERT_TILE, side="right")
    tile_expert = jnp.minimum(tile_expert, _N_EXPERTS - 1).astype(jnp.int32)
    n_used = (ends[-1:] // _EXPERT_TILE).astype(jnp.int32)
    half = _EXPERT_TILE // 2
    tail_start = ends[-1] + jnp.arange(2 * _N_EXPERTS, dtype=jnp.int32) * half
    tail_len = jnp.where(tail_start < n_tiles * _EXPERT_TILE, half, 0)
    pad_start = jnp.concatenate([offsets + counts, tail_start]).astype(jnp.int32)
    pad_len = jnp.concatenate([padded - counts, tail_len]).astype(jnp.int32)
    return pos.T.reshape(-1), tile_expert, n_used, pad_start, pad_len


def _ple_kernel(hb_ref, h_ref, p_ref, wpg_ref, wple_ref, o32_ref, o16_ref):
    gate = jax.nn.sigmoid(jnp.dot(hb_ref[...], wpg_ref[...], preferred_element_type=_F32))
    pe = jnp.dot(p_ref[...].astype(_BF16), wple_ref[...], preferred_element_type=_F32)
    o = h_ref[...] + gate * pe
    o32_ref[...] = o
    o16_ref[...] = o.astype(_BF16)


def _ple(hb, h, p, wpg, wple, tm, tn):
    m, d = hb.shape
    pd = p.shape[1]
    tm, tn = min(tm, m), min(tn, d)
    out_tile = pl.BlockSpec((tm, tn), lambda i, j: (i, j))
    return pl.pallas_call(
        _ple_kernel,
        out_shape=(jax.ShapeDtypeStruct((m, d), _F32), jax.ShapeDtypeStruct((m, d), _BF16)),
        grid=(m // tm, d // tn),
        in_specs=[pl.BlockSpec((tm, d), lambda i, j: (i, 0)),
                  out_tile,
                  pl.BlockSpec((tm, pd), lambda i, j: (i, 0)),
                  pl.BlockSpec((d, tn), lambda i, j: (0, j)),
                  pl.BlockSpec((pd, tn), lambda i, j: (0, j))],
        out_specs=(out_tile, out_tile),
        compiler_params=_params(2),
        name="ple_gate",
    )(hb, h, p, wpg, wple)


def _layer(x, xb, p, image_rows, alpha, w):
    (w_in, attn_rpb, gmlp_ln_g, gmlp_ln_b, gmlp_ws, gmlp_bs, mix_norm_g, w_out, ln1_g, ln1_b,
     w_router, router_bias, w_e_gate, w_e_up, w_e_down, w_sh_gate, w_sh_up, w_sh_down,
     ln2_g, ln2_b, w_ple, w_ple_gate) = w
    row = lambda v: v.reshape(1, -1).astype(_F32)

    proj = _matmul(xb, w_in.astype(_BF16), _BF16, 1024, 1024, "in_proj")

    rs_tbl, d0_tbl = _attn_tables(image_rows)
    a = _attention(proj, _attn_bias_table(attn_rpb), row(mix_norm_g[:_ATTN_W]),
                   jnp.asarray(rs_tbl), jnp.asarray(d0_tbl))

    bsx = jnp.repeat(gmlp_bs.T.astype(_F32), _HEAD_DIM, axis=1)
    g = _gmlp(proj, gmlp_ws.astype(_BF16), row(gmlp_ln_g), row(gmlp_ln_b), bsx, row(mix_norm_g[_ATTN_W:]))

    mixed_out = _out_proj(a, g, w_out.astype(_BF16), 1024, 1024)
    h, hb, hpk = _add_ln(x, mixed_out, row(ln1_g), row(ln1_b), alpha, 256)

    wr_p = jnp.pad(w_router.astype(_F32), ((0, 0), (0, _LANES - _N_EXPERTS)))
    bias_p = jnp.broadcast_to(router_bias.astype(_F32)[:, None], (_N_EXPERTS, _LANES))
    e_t, w_t, r_t, counts = _router(h, wr_p, bias_p, 512)

    n_rows = x.shape[0] * _TOP_K + _N_EXPERTS * _EXPERT_TILE
    pos_flat, tile_expert, n_used, pad_start, pad_len = _routing_plan(
        e_t, r_t, counts[:, 0], n_rows // _EXPERT_TILE)
    xs = _dispatch(hpk, pos_flat, pad_start, pad_len, n_rows)
    wgu = jnp.concatenate([w_e_gate, w_e_up], axis=-1).astype(_BF16)
    ys = _experts(xs, wgu, w_e_down.astype(_BF16), tile_expert, n_used)
    shared = _mlp(hb, w_sh_gate.astype(_BF16), w_sh_up.astype(_BF16), w_sh_down.astype(_BF16), 512, 512)
    h2, h2b = _combine(ys, pos_flat, w_t.T, shared, h, row(ln2_g), row(ln2_b), alpha)
    return _ple(h2b, h2, p, w_ple_gate.astype(_BF16), w_ple.astype(_BF16), 1024, 512)


def kernel(x_prompt, x_sample, p_prompt, p_sample, w_in, attn_rpb, gmlp_ln_g, gmlp_ln_b, gmlp_ws, gmlp_bs,
           mix_norm_g, w_out, ln1_g, ln1_b, w_router, router_bias, w_e_gate, w_e_up, w_e_down,
           w_sh_gate, w_sh_up, w_sh_down, ln2_g, ln2_b, w_ple, w_ple_gate):
    depth = w_in.shape[0]
    d = x_prompt.shape[-1]
    alpha = (2 * depth) ** 0.25
    groups = (x_prompt, x_sample)
    image_rows = [xg.shape[1] // _GRID_W for xg in groups for _ in range(xg.shape[0])]
    x = jnp.concatenate([xg.reshape(-1, d) for xg in groups], axis=0)
    xb = x.astype(_BF16)
    weights = (w_in, attn_rpb, gmlp_ln_g, gmlp_ln_b, gmlp_ws, gmlp_bs, mix_norm_g, w_out, ln1_g, ln1_b,
               w_router, router_bias, w_e_gate, w_e_up, w_e_down, w_sh_gate, w_sh_up, w_sh_down,
               ln2_g, ln2_b, w_ple, w_ple_gate)
    for i in range(depth):
        p = jnp.concatenate([pg[i].reshape(-1, pg.shape[-1]) for pg in (p_prompt, p_sample)], axis=0)
        x, xb = _layer(x, xb, p, image_rows, alpha, tuple(wt[i] for wt in weights))
    n_prompt = x_prompt.shape[0] * x_prompt.shape[1]
    return x[:n_prompt].reshape(x_prompt.shape), x[n_prompt:].reshape(x_sample.shape)
```

```python
import functools

import numpy as np
import jax
import jax.numpy as jnp
from jax import lax
from jax.experimental import pallas as pl
from jax.experimental.pallas import tpu as pltpu

_F32 = jnp.float32
_BF16 = jnp.bfloat16
_U32 = jnp.uint32

_HEAD_DIM = 128
_N_HEADS = 16
_N_GROUPS = 16
_ATTN_W = _N_HEADS * _HEAD_DIM
_GMLP_W = _N_GROUPS * _HEAD_DIM
_CHUNK = 128
_GRID_W = 64
_WIN_H = 8
_WIN_W = 16
_N_EXPERTS = 64
_TOP_K = 8
_N_EGROUPS = 8
_EGROUP = _N_EXPERTS // _N_EGROUPS
_TOPK_GROUPS = 4
_EXPERT_FF = 128
_ROUTED_SCALE = 2.5
_LN_EPS = 1e-5
_NEG_INF = -1e30
_LANES = 128
_EXPERT_TILE = 256
_GATHER_TOKENS = 128
_SLAB = 16
_BUF_PITCH = 24

_V7X_VMEM_LIMIT = 56 * 1024 * 1024


def _params(n_axes):
    return pltpu.CompilerParams(dimension_semantics=("arbitrary",) * n_axes,
                                vmem_limit_bytes=_V7X_VMEM_LIMIT)


def _mm_kernel(x_ref, w_ref, o_ref):
    o_ref[...] = jnp.dot(x_ref[...], w_ref[...], preferred_element_type=_F32).astype(o_ref.dtype)


def _matmul(x, w, layer, out_dtype, tm, tn, name):
    m, k = x.shape
    n = w.shape[2]
    tm, tn = min(tm, m), min(tn, n)
    return pl.pallas_call(
        _mm_kernel,
        out_shape=jax.ShapeDtypeStruct((m, n), out_dtype),
        grid=(m // tm, n // tn),
        in_specs=[pl.BlockSpec((tm, k), lambda i, j: (i, 0)),
                  pl.BlockSpec((None, k, tn), lambda i, j: (layer, 0, j))],
        out_specs=pl.BlockSpec((tm, tn), lambda i, j: (i, j)),
        compiler_params=_params(2),
        name=name,
    )(x, w)


def _mm2_kernel(a_ref, g_ref, wa_ref, wg_ref, o_ref):
    acc = jnp.dot(a_ref[...], wa_ref[...], preferred_element_type=_F32)
    acc += jnp.dot(g_ref[...], wg_ref[...], preferred_element_type=_F32)
    o_ref[...] = acc


def _out_proj(a, g, w_out, layer, tm, tn):
    m, ka = a.shape
    kg = g.shape[1]
    n = w_out.shape[2]
    tm, tn = min(tm, m), min(tn, n)
    return pl.pallas_call(
        _mm2_kernel,
        out_shape=jax.ShapeDtypeStruct((m, n), _F32),
        grid=(m // tm, n // tn),
        in_specs=[pl.BlockSpec((tm, ka), lambda i, j: (i, 0)),
                  pl.BlockSpec((tm, kg), lambda i, j: (i, 0)),
                  pl.BlockSpec((None, ka, tn), lambda i, j: (layer, 0, j)),
                  pl.BlockSpec((None, kg, tn), lambda i, j: (layer, 1, j))],
        out_specs=pl.BlockSpec((tm, tn), lambda i, j: (i, j)),
        compiler_params=_params(2),
        name="out_proj",
    )(a, g, w_out, w_out)


def _attn_kernel(rs_ref, d0_ref, q_ref, *rest):
    del rs_ref, d0_ref
    k_refs = rest[:_WIN_H]
    v_refs = rest[_WIN_H:2 * _WIN_H]
    bias_ref, gain_ref, o_ref, acc_ref = rest[2 * _WIN_H:]
    scale = _HEAD_DIM ** -0.5
    ss = jnp.zeros((_GRID_W, 1), _F32)
    for h in range(_N_HEADS):
        sl = slice(h * _HEAD_DIM, (h + 1) * _HEAD_DIM)
        q = q_ref[:, sl]
        k = jnp.concatenate([r[:, sl] for r in k_refs], axis=0)
        v = jnp.concatenate([r[:, sl] for r in v_refs], axis=0)
        s = lax.dot_general(q, k, (((1,), (1,)), ((), ())), preferred_element_type=_F32)
        s = s * scale + bias_ref[h]
        m = jnp.max(s, axis=-1, keepdims=True)
        p = jnp.exp(s - m)
        l = jnp.sum(p, axis=-1, keepdims=True)
        o = jnp.dot(p.astype(_BF16), v, preferred_element_type=_F32) / l
        acc_ref[:, sl] = o
        ss = ss + jnp.sum(o * o, axis=-1, keepdims=True)
    r = lax.rsqrt(ss / _ATTN_W + _LN_EPS)
    o_ref[...] = (acc_ref[...] * r * gain_ref[...]).astype(o_ref.dtype)


def _attention(proj, bias_tbl, gain_a, rs_tbl, d0_tbl):
    n = proj.shape[0]
    nr = n // _GRID_W
    proj3 = proj.reshape(nr, _GRID_W, proj.shape[1])
    blk = (None, _GRID_W, _ATTN_W)
    in_specs = [pl.BlockSpec(blk, lambda r, rs, d0: (r, 0, 0))]
    for c in (1, 2):
        for i in range(_WIN_H):
            in_specs.append(pl.BlockSpec(blk, lambda r, rs, d0, i=i, c=c: (rs[r] + i, 0, c)))
    in_specs.append(pl.BlockSpec((None, _N_HEADS, _GRID_W, _WIN_H * _GRID_W),
                                 lambda r, rs, d0: (d0[r], 0, 0, 0)))
    in_specs.append(pl.BlockSpec((1, _ATTN_W), lambda r, rs, d0: (0, 0)))
    out = pl.pallas_call(
        _attn_kernel,
        out_shape=jax.ShapeDtypeStruct((nr, _GRID_W, _ATTN_W), _BF16),
        grid_spec=pltpu.PrefetchScalarGridSpec(
            num_scalar_prefetch=2,
            grid=(nr,),
            in_specs=in_specs,
            out_specs=pl.BlockSpec(blk, lambda r, rs, d0: (r, 0, 0)),
            scratch_shapes=[pltpu.VMEM((_GRID_W, _ATTN_W), _F32)],
        ),
        compiler_params=_params(1),
        name="na2d_attention",
    )(rs_tbl, d0_tbl, *([proj3] * (1 + 2 * _WIN_H)), bias_tbl, gain_a)
    return out.reshape(n, _ATTN_W)


def _attn_tables(image_rows):
    rs_tbl, d0_tbl, base = [], [], 0
    for rows in image_rows:
        kh = min(_WIN_H, rows)
        assert kh == _WIN_H
        for r in range(rows):
            rs = min(max(r - kh // 2, 0), rows - kh)
            rs_tbl.append(base + rs)
            d0_tbl.append(rs - r + _WIN_H - 1)
        base += rows
    return np.asarray(rs_tbl, np.int32), np.asarray(d0_tbl, np.int32)


def _attn_bias_table(rpb):
    cols = np.arange(_GRID_W)
    col_start = np.clip(cols - _WIN_W // 2, 0, _GRID_W - _WIN_W)
    col_mask = (cols[None, :] >= col_start[:, None]) & (cols[None, :] < col_start[:, None] + _WIN_W)
    col_off = np.clip(cols[None, :] - cols[:, None] + (_WIN_W - 1), 0, 2 * _WIN_W - 2)
    rpb_c = jnp.where(col_mask[None, None], rpb[:, :, col_off].astype(_F32), _F32(_NEG_INF))
    tabs = []
    for d0 in range(_WIN_H):
        t = rpb_c[:, d0:d0 + _WIN_H].transpose(0, 2, 1, 3)
        tabs.append(t.reshape(_N_HEADS, _GRID_W, _WIN_H * _GRID_W))
    return jnp.stack(tabs)


def _gmlp_kernel(u_ref, v_ref, ws_ref, lng_ref, lnb_ref, bsx_ref, gain_ref, o_ref, acc_ref):
    ss = jnp.zeros((_CHUNK, 1), _F32)
    for g in range(_N_GROUPS):
        sl = slice(g * _HEAD_DIM, (g + 1) * _HEAD_DIM)
        v = jax.nn.gelu(v_ref[:, sl].astype(_F32))
        mu = jnp.mean(v, axis=-1, keepdims=True)
        c = v - mu
        var = jnp.mean(c * c, axis=-1, keepdims=True)
        vn = c * lax.rsqrt(var + _LN_EPS) * lng_ref[:, sl] + lnb_ref[:, sl]
        mixed = jnp.dot(ws_ref[g], vn.astype(_BF16), preferred_element_type=_F32) + bsx_ref[:, sl]
        o = jax.nn.gelu(u_ref[:, sl].astype(_F32)) * mixed
        acc_ref[:, sl] = o
        ss = ss + jnp.sum(o * o, axis=-1, keepdims=True)
    r = lax.rsqrt(ss / _GMLP_W + _LN_EPS)
    o_ref[...] = (acc_ref[...] * r * gain_ref[...]).astype(o_ref.dtype)


def _gmlp(proj, ws, ln_g, ln_b, bsx, gain_g):
    n = proj.shape[0]
    row = pl.BlockSpec((1, _GMLP_W), lambda i: (0, 0))
    return pl.pallas_call(
        _gmlp_kernel,
        out_shape=jax.ShapeDtypeStruct((n, _GMLP_W), _BF16),
        grid=(n // _CHUNK,),
        in_specs=[pl.BlockSpec((_CHUNK, _GMLP_W), lambda i: (i, 3)),
                  pl.BlockSpec((_CHUNK, _GMLP_W), lambda i: (i, 4)),
                  pl.BlockSpec((_N_GROUPS, _CHUNK, _CHUNK), lambda i: (0, 0, 0)),
                  row, row,
                  pl.BlockSpec((_CHUNK, _GMLP_W), lambda i: (0, 0)),
                  row],
        out_specs=pl.BlockSpec((_CHUNK, _GMLP_W), lambda i: (i, 0)),
        scratch_shapes=[pltpu.VMEM((_CHUNK, _GMLP_W), _F32)],
        compiler_params=_params(1),
        name="gmlp_gating",
    )(proj, proj, ws, ln_g, ln_b, bsx, gain_g)


def _layer_norm(z, g, b):
    mu = jnp.mean(z, axis=-1, keepdims=True)
    c = z - mu
    var = jnp.mean(c * c, axis=-1, keepdims=True)
    return c * lax.rsqrt(var + _LN_EPS) * g + b


def _pack_halves(x):
    k = x.shape[1] // 2
    bits = lambda v: lax.bitcast_convert_type(v.astype(_BF16).astype(_F32), _U32)
    return (bits(x[:, k:]) & _U32(0xFFFF0000)) | (bits(x[:, :k]) >> _U32(16))


def _unpack_halves(words):
    lo = lax.bitcast_convert_type(words << _U32(16), _F32)
    hi = lax.bitcast_convert_type(words & _U32(0xFFFF0000), _F32)
    return lo, hi


def _store_slabs(ref, first, words):
    m = words.shape[0]
    for s in range(_SLAB):
        ref[pl.ds(first * _SLAB + s, m, stride=_SLAB), :] = words[:, s * _LANES:(s + 1) * _LANES]


def _load_slabs(ref, first, m, pitch=_SLAB):
    return jnp.concatenate([ref[pl.ds(first * pitch + s, m, stride=pitch), :] for s in range(_SLAB)], axis=1)


def _slab_copy(src_ref, src_slab, dst_ref, dst_slab, sem, dst_pitch=_SLAB):
    src = src_ref.at[pl.ds(pl.multiple_of(src_slab * _SLAB, _SLAB), _SLAB)]
    dst = dst_ref.at[pl.ds(pl.multiple_of(dst_slab * dst_pitch, 8), _SLAB)]
    return pltpu.make_async_copy(src, dst, sem)


def _wait_slabs(src_ref, dst_ref, n_slabs, sem):
    pltpu.make_async_copy(src_ref.at[pl.ds(0, n_slabs * _SLAB)], dst_ref.at[pl.ds(0, n_slabs * _SLAB)], sem).wait()


def _add_ln_kernel(alpha, res_ref, y_ref, g_ref, b_ref, o32_ref, o16_ref, opk_ref):
    o = _layer_norm(alpha * res_ref[...] + y_ref[...], g_ref[...], b_ref[...])
    o32_ref[...] = o
    o16_ref[...] = o.astype(_BF16)
    _store_slabs(opk_ref, 0, _pack_halves(o))


def _add_ln(res, y, g, b, alpha, tm):
    m, d = res.shape
    tm = min(tm, m)
    tile = pl.BlockSpec((tm, d), lambda i: (i, 0))
    half = pl.BlockSpec((tm * _SLAB, _LANES), lambda i: (i, 0))
    row = pl.BlockSpec((1, d), lambda i: (0, 0))
    return pl.pallas_call(
        functools.partial(_add_ln_kernel, alpha),
        out_shape=(jax.ShapeDtypeStruct((m, d), _F32), jax.ShapeDtypeStruct((m, d), _BF16),
                   jax.ShapeDtypeStruct((m * _SLAB, _LANES), _U32)),
        grid=(m // tm,),
        in_specs=[tile, tile, row, row],
        out_specs=(tile, tile, half),
        compiler_params=_params(1),
        name="residual_layernorm",
    )(res, y, g, b)


def _first_argmax(x, iota, n):
    m = jnp.max(x, axis=0, keepdims=True)
    return jnp.min(jnp.where(x == m, iota, n), axis=0, keepdims=True)


def _router_kernel(h_ref, wr_ref, bias_ref, tri_ref, e_ref, w_ref, r_ref, cnt_ref, carry_ref):
    tm = h_ref.shape[0]

    @pl.when(pl.program_id(0) == 0)
    def _():
        carry_ref[...] = jnp.zeros_like(carry_ref)

    logits = jnp.dot(h_ref[...], wr_ref[...], precision=lax.Precision.HIGHEST,
                     preferred_element_type=_F32)
    scores = jax.nn.sigmoid(logits.T[:_N_EXPERTS])
    bias = jnp.concatenate([bias_ref[...]] * (tm // _LANES), axis=1)
    sel = scores + bias
    ninf = _F32(-jnp.inf)

    iota_g = lax.broadcasted_iota(jnp.int32, (_EGROUP, tm), 0)
    grp = []
    for g in range(_N_EGROUPS):
        x = sel[g * _EGROUP:(g + 1) * _EGROUP]
        m1 = jnp.max(x, axis=0, keepdims=True)
        hit = iota_g == _first_argmax(x, iota_g, _EGROUP)
        m2 = jnp.max(jnp.where(hit, ninf, x), axis=0, keepdims=True)
        grp.append(m1 + m2)
    grp = jnp.concatenate(grp, axis=0)

    iota_n = lax.broadcasted_iota(jnp.int32, (_N_EGROUPS, tm), 0)
    gmask = jnp.zeros((_N_EGROUPS, tm), _F32)
    for _ in range(_TOPK_GROUPS):
        hit = iota_n == _first_argmax(grp, iota_n, _N_EGROUPS)
        gmask = jnp.where(hit, 1.0, gmask)
        grp = jnp.where(hit, ninf, grp)

    masked = jnp.concatenate(
        [jnp.where(gmask[g:g + 1] > 0, sel[g * _EGROUP:(g + 1) * _EGROUP], ninf) for g in range(_N_EGROUPS)], axis=0)
    iota_e = lax.broadcasted_iota(jnp.int32, (_N_EXPERTS, tm), 0)
    chosen = jnp.zeros((_N_EXPERTS, tm), _F32)
    picks = []
    for _ in range(_TOP_K):
        idx = _first_argmax(masked, iota_e, _N_EXPERTS)
        hit = iota_e == idx
        chosen = jnp.where(hit, 1.0, chosen)
        masked = jnp.where(hit, ninf, masked)
        picks.append(idx)

    w = scores * chosen
    gates = w / jnp.sum(w, axis=0, keepdims=True) * _ROUTED_SCALE

    carry = carry_ref[...]
    rank = carry[:, 0:1] + jnp.dot(chosen.astype(_BF16), tri_ref[...], preferred_element_type=_F32)
    carry = carry + jnp.sum(chosen, axis=1, keepdims=True)
    carry_ref[...] = carry
    cnt_ref[...] = carry

    pick_w, pick_r = [], []
    for idx in picks:
        hit = iota_e == idx
        pick_w.append(jnp.sum(jnp.where(hit, gates, 0.0), axis=0, keepdims=True))
        pick_r.append(jnp.sum(jnp.where(hit, rank, 0.0), axis=0, keepdims=True))
    e_ref[...] = jnp.concatenate(picks, axis=0)
    w_ref[...] = jnp.concatenate(pick_w, axis=0)
    r_ref[...] = jnp.concatenate(pick_r, axis=0).astype(jnp.int32)


def _router(h, w_router_p, bias_p, tm):
    m, d = h.shape
    tm = min(tm, m)
    tri = jnp.asarray(np.triu(np.ones((tm, tm), np.float32), 1), _BF16)
    slot = pl.BlockSpec((_TOP_K, tm), lambda i: (0, i))
    return pl.pallas_call(
        _router_kernel,
        out_shape=(jax.ShapeDtypeStruct((_TOP_K, m), jnp.int32), jax.ShapeDtypeStruct((_TOP_K, m), _F32),
                   jax.ShapeDtypeStruct((_TOP_K, m), jnp.int32), jax.ShapeDtypeStruct((_N_EXPERTS, _LANES), _F32)),
        grid=(m // tm,),
        in_specs=[pl.BlockSpec((tm, d), lambda i: (i, 0)),
                  pl.BlockSpec((d, _LANES), lambda i: (0, 0)),
                  pl.BlockSpec((_N_EXPERTS, _LANES), lambda i: (0, 0)),
                  pl.BlockSpec((tm, tm), lambda i: (0, 0))],
        out_specs=(slot, slot, slot, pl.BlockSpec((_N_EXPERTS, _LANES), lambda i: (0, 0))),
        scratch_shapes=[pltpu.VMEM((_N_EXPERTS, _LANES), _F32)],
        compiler_params=_params(1),
        name="moe_router",
    )(h, w_router_p, bias_p, tri)


def _pad_chunks():
    return [1 << b for b in range(_EXPERT_TILE.bit_length() - 1)]


def _dispatch_kernel(pad_start_ref, pad_len_ref, pos_ref, x_ref, xs_ref, zero_ref, sem, zsem):
    tm = x_ref.shape[0] // _SLAB

    def start(t, c):
        for k in range(_TOP_K):
            _slab_copy(x_ref, t, xs_ref, pos_ref[t * _TOP_K + k], sem).start()
        return c

    lax.fori_loop(0, tm, start, 0)

    @pl.when(pl.program_id(0) == 0)
    def _():
        zero_ref[...] = jnp.zeros_like(zero_ref)

        def fill(e, do_start):
            cur = pad_start_ref[e]
            n = pad_len_ref[e]
            for c in _pad_chunks():
                @pl.when((n & c) != 0)
                def _(cur=cur, c=c):
                    dst = xs_ref.at[pl.ds(pl.multiple_of(cur * _SLAB, _SLAB), c * _SLAB)]
                    cp = pltpu.make_async_copy(zero_ref.at[pl.ds(0, c * _SLAB)], dst, zsem)
                    if do_start:
                        cp.start()
                    else:
                        cp.wait()
                cur = cur + (n & c)

        n_pads = pad_start_ref.shape[0]
        lax.fori_loop(0, n_pads, lambda e, c: (fill(e, True), c)[1], 0)
        lax.fori_loop(0, n_pads, lambda e, c: (fill(e, False), c)[1], 0)

    for _ in range(_TOP_K):
        _wait_slabs(x_ref, xs_ref, tm, sem)


def _dispatch(xpk, pos_flat, pad_start, pad_len, n_rows):
    tm = _GATHER_TOKENS
    return pl.pallas_call(
        _dispatch_kernel,
        out_shape=jax.ShapeDtypeStruct((n_rows * _SLAB, _LANES), _U32),
        grid_spec=pltpu.PrefetchScalarGridSpec(
            num_scalar_prefetch=2,
            grid=(xpk.shape[0] // (tm * _SLAB),),
            in_specs=[pl.BlockSpec((tm * _TOP_K,), lambda i, ps, pn: (i,), memory_space=pltpu.SMEM),
                      pl.BlockSpec((tm * _SLAB, _LANES), lambda i, ps, pn: (i, 0))],
            out_specs=pl.BlockSpec(memory_space=pl.ANY),
            scratch_shapes=[pltpu.VMEM((_EXPERT_TILE // 2 * _SLAB, _LANES), _U32),
                            pltpu.SemaphoreType.DMA(()), pltpu.SemaphoreType.DMA(())],
        ),
        compiler_params=_params(1),
        name="moe_dispatch",
    )(pad_start, pad_len, pos_flat, xpk)


def _experts_kernel(te_ref, nu_ref, xs_ref, wg_ref, wu_ref, wd_ref, ys_ref):
    @pl.when(pl.program_id(0) >= nu_ref[0])
    def _():
        ys_ref[...] = jnp.zeros_like(ys_ref)

    @pl.when(pl.program_id(0) < nu_ref[0])
    def _():
        lo, hi = _unpack_halves(_load_slabs(xs_ref, 0, _EXPERT_TILE))
        k = lo.shape[1]
        w_lo = jnp.concatenate([wg_ref[:k], wu_ref[:k]], axis=1)
        w_hi = jnp.concatenate([wg_ref[k:], wu_ref[k:]], axis=1)
        gu = jnp.dot(lo.astype(_BF16), w_lo, preferred_element_type=_F32)
        gu += jnp.dot(hi.astype(_BF16), w_hi, preferred_element_type=_F32)
        act = jax.nn.silu(gu[:, :_EXPERT_FF]) * gu[:, _EXPERT_FF:]
        y = jnp.dot(act.astype(_BF16), wd_ref[...], preferred_element_type=_F32)
        _store_slabs(ys_ref, 0, _pack_halves(y))


def _experts(xs, wg, wu, wd, layer, tile_expert, n_used):
    n_rows = xs.shape[0] // _SLAB
    d = wd.shape[3]
    w_in = pl.BlockSpec((None, None, d, _EXPERT_FF), lambda t, te, nu: (layer, te[t], 0, 0))
    tile = lambda t, te, nu: (t, 0)
    return pl.pallas_call(
        _experts_kernel,
        out_shape=jax.ShapeDtypeStruct(xs.shape, _U32),
        grid_spec=pltpu.PrefetchScalarGridSpec(
            num_scalar_prefetch=2,
            grid=(n_rows // _EXPERT_TILE,),
            in_specs=[pl.BlockSpec((_EXPERT_TILE * _SLAB, _LANES), tile),
                      w_in, w_in,
                      pl.BlockSpec((None, None, _EXPERT_FF, d), lambda t, te, nu: (layer, te[t], 0, 0))],
            out_specs=pl.BlockSpec((_EXPERT_TILE * _SLAB, _LANES), tile),
        ),
        compiler_params=_params(1),
        name="moe_experts",
    )(tile_expert, n_used, xs, wg, wu, wd)


def _mlp_kernel(x_ref, wg_ref, wu_ref, wd_ref, o_ref):
    j = pl.program_id(1)
    x = x_ref[...]
    hg = jnp.dot(x, wg_ref[...], preferred_element_type=_F32)
    hu = jnp.dot(x, wu_ref[...], preferred_element_type=_F32)
    contrib = jnp.dot((jax.nn.silu(hg) * hu).astype(_BF16), wd_ref[...], preferred_element_type=_F32)

    @pl.when(j == 0)
    def _():
        o_ref[...] = contrib

    @pl.when(j > 0)
    def _():
        o_ref[...] += contrib


def _mlp(xb, wg, wu, wd, layer, tm, tf):
    m, d = xb.shape
    ff = wg.shape[2]
    tm, tf = min(tm, m), min(tf, ff)
    return pl.pallas_call(
        _mlp_kernel,
        out_shape=jax.ShapeDtypeStruct((m, d), _F32),
        grid=(m // tm, ff // tf),
        in_specs=[pl.BlockSpec((tm, d), lambda i, j: (i, 0)),
                  pl.BlockSpec((None, d, tf), lambda i, j: (layer, 0, j)),
                  pl.BlockSpec((None, d, tf), lambda i, j: (layer, 0, j)),
                  pl.BlockSpec((None, tf, d), lambda i, j: (layer, j, 0))],
        out_specs=pl.BlockSpec((tm, d), lambda i, j: (i, 0)),
        compiler_params=_params(2),
        name="shared_expert",
    )(xb, wg, wu, wd)


def _combine_kernel(alpha, pos_ref, w_ref, ys_ref, sh_ref, h_ref, g_ref, b_ref, o32_ref, o16_ref, buf_ref, sem):
    tm = h_ref.shape[0]

    def start(t, c):
        for k in range(_TOP_K):
            _slab_copy(ys_ref, pos_ref[t * _TOP_K + k], buf_ref, k * tm + t, sem, _BUF_PITCH).start()
        return c

    lax.fori_loop(0, tm, start, 0)
    _wait_slabs(ys_ref, buf_ref, _TOP_K * tm, sem)

    acc_lo = acc_hi = None
    for k in range(_TOP_K):
        lo, hi = _unpack_halves(_load_slabs(buf_ref, k * tm, tm, _BUF_PITCH))
        wk = w_ref[:, k:k + 1]
        acc_lo = lo * wk if acc_lo is None else acc_lo + lo * wk
        acc_hi = hi * wk if acc_hi is None else acc_hi + hi * wk
    moe = jnp.concatenate([acc_lo, acc_hi], axis=1) + sh_ref[...]
    o = _layer_norm(alpha * h_ref[...] + moe, g_ref[...], b_ref[...])
    o32_ref[...] = o
    o16_ref[...] = o.astype(_BF16)


def _combine(ys, pos_flat, w_tok, shared, h, g, b, alpha):
    m, d = h.shape
    tm = _GATHER_TOKENS
    tile = pl.BlockSpec((tm, d), lambda i: (i, 0))
    row = pl.BlockSpec((1, d), lambda i: (0, 0))
    return pl.pallas_call(
        functools.partial(_combine_kernel, alpha),
        out_shape=(jax.ShapeDtypeStruct((m, d), _F32), jax.ShapeDtypeStruct((m, d), _BF16)),
        grid=(m // tm,),
        in_specs=[pl.BlockSpec((tm * _TOP_K,), lambda i: (i,), memory_space=pltpu.SMEM),
                  pl.BlockSpec((tm, _TOP_K), lambda i: (i, 0)),
                  pl.BlockSpec(memory_space=pl.ANY),
                  tile, tile, row, row],
        out_specs=(tile, tile),
        scratch_shapes=[pltpu.VMEM((_TOP_K * tm * _BUF_PITCH, _LANES), _U32), pltpu.SemaphoreType.DMA(())],
        compiler_params=_params(1),
        name="moe_combine_layernorm",
    )(pos_flat, w_tok, ys, shared, h, g, b)


def _routing_plan(e_t, r_t, counts, n_tiles):
    counts = counts.astype(jnp.int32)
    padded = (counts + _EXPERT_TILE - 1) // _EXPERT_TILE * _EXPERT_TILE
    ends = jnp.cumsum(padded)
    offsets = ends - padded
    experts = jnp.arange(_N_EXPERTS, dtype=jnp.int32)
    pos = r_t + jnp.sum(jnp.where(e_t[..., None] == experts, offsets, 0), axis=-1)
    tile_start = jnp.arange(n_tiles, dtype=jnp.int32) * _EXPERT_TILE
    tile_expert = jnp.sum((ends[None, :] <= tile_start[:, None]).astype(jnp.int32), axis=1)
    tile_expert = jnp.minimum(tile_expert, _N_EXPERTS - 1)
    n_used = (ends[-1:] // _EXPERT_TILE).astype(jnp.int32)
    half = _EXPERT_TILE // 2
    tail_start = ends[-1] + jnp.arange(2 * _N_EXPERTS, dtype=jnp.int32) * half
    tail_len = jnp.where(tail_start < n_tiles * _EXPERT_TILE, half, 0)
    pad_start = jnp.concatenate([offsets + counts, tail_start]).astype(jnp.int32)
    pad_len = jnp.concatenate([padded - counts, tail_len]).astype(jnp.int32)
    return pos.T.reshape(-1), tile_expert, n_used, pad_start, pad_len


def _ple_kernel(hb_ref, h_ref, p_ref, wpg_ref, wple_ref, o32_ref, o16_ref):
    gate = jax.nn.sigmoid(jnp.dot(hb_ref[...], wpg_ref[...], preferred_element_type=_F32))
    pe = jnp.dot(p_ref[...].astype(_BF16), wple_ref[...], preferred_element_type=_F32)
    o = h_ref[...] + gate * pe
    o32_ref[...] = o
    o16_ref[...] = o.astype(_BF16)


def _ple(hb, h, p, wpg, wple, layer, row0, rows, tm, tn):
    d = hb.shape[1]
    pd = p.shape[1]
    tm, tn = min(tm, rows), min(tn, d)
    i0 = row0 // tm
    assert row0 % tm == 0 and rows % tm == 0
    out_tile = pl.BlockSpec((tm, tn), lambda i, j: (i, j))
    return pl.pallas_call(
        _ple_kernel,
        out_shape=(jax.ShapeDtypeStruct((rows, d), _F32), jax.ShapeDtypeStruct((rows, d), _BF16)),
        grid=(rows // tm, d // tn),
        in_specs=[pl.BlockSpec((tm, d), lambda i, j: (i0 + i, 0)),
                  pl.BlockSpec((tm, tn), lambda i, j: (i0 + i, j)),
                  pl.BlockSpec((tm, pd), lambda i, j: (i0 + i, 0)),
                  pl.BlockSpec((None, d, tn), lambda i, j: (layer, 0, j)),
                  pl.BlockSpec((None, pd, tn), lambda i, j: (layer, 0, j))],
        out_specs=(out_tile, out_tile),
        compiler_params=_params(2),
        name="ple_gate",
    )(hb, h, p, wpg, wple)


def _layer(x, xb, p, image_rows, alpha, layer, out_rows, w, wb):
    (attn_rpb, gmlp_ln_g, gmlp_ln_b, gmlp_bs, mix_norm_g, ln1_g, ln1_b, w_router, router_bias, ln2_g, ln2_b) = w
    (w_in, gmlp_ws, w_out, w_e_gate, w_e_up, w_e_down, w_sh_gate, w_sh_up, w_sh_down, w_ple, w_ple_gate) = wb
    row = lambda v: v.reshape(1, -1).astype(_F32)

    proj = _matmul(xb, w_in, layer, _BF16, 1024, 1024, "in_proj")

    rs_tbl, d0_tbl = _attn_tables(image_rows)
    a = _attention(proj, _attn_bias_table(attn_rpb), row(mix_norm_g[:_ATTN_W]),
                   jnp.asarray(rs_tbl), jnp.asarray(d0_tbl))

    bsx = jnp.repeat(gmlp_bs.T.astype(_F32), _HEAD_DIM, axis=1)
    g = _gmlp(proj, gmlp_ws[layer], row(gmlp_ln_g), row(gmlp_ln_b), bsx, row(mix_norm_g[_ATTN_W:]))

    mixed_out = _out_proj(a, g, w_out, layer, 1024, 1024)
    h, hb, hpk = _add_ln(x, mixed_out, row(ln1_g), row(ln1_b), alpha, 256)

    wr_p = jnp.pad(w_router.astype(_F32), ((0, 0), (0, _LANES - _N_EXPERTS)))
    bias_p = jnp.broadcast_to(router_bias.astype(_F32)[:, None], (_N_EXPERTS, _LANES))
    e_t, w_t, r_t, counts = _router(h, wr_p, bias_p, 512)

    n_rows = x.shape[0] * _TOP_K + _N_EXPERTS * _EXPERT_TILE
    pos_flat, tile_expert, n_used, pad_start, pad_len = _routing_plan(
        e_t, r_t, counts[:, 0], n_rows // _EXPERT_TILE)
    xs = _dispatch(hpk, pos_flat, pad_start, pad_len, n_rows)
    ys = _experts(xs, w_e_gate, w_e_up, w_e_down, layer, tile_expert, n_used)
    shared = _mlp(hb, w_sh_gate, w_sh_up, w_sh_down, layer, 512, 512)
    h2, h2b = _combine(ys, pos_flat, w_t.T, shared, h, row(ln2_g), row(ln2_b), alpha)
    return [_ple(h2b, h2, p, w_ple_gate, w_ple, layer, r0, nr, 1024, 512) for r0, nr in out_rows]


def kernel(x_prompt, x_sample, p_prompt, p_sample, w_in, attn_rpb, gmlp_ln_g, gmlp_ln_b, gmlp_ws, gmlp_bs,
           mix_norm_g, w_out, ln1_g, ln1_b, w_router, router_bias, w_e_gate, w_e_up, w_e_down,
           w_sh_gate, w_sh_up, w_sh_down, ln2_g, ln2_b, w_ple, w_ple_gate):
    depth = w_in.shape[0]
    d = x_prompt.shape[-1]
    alpha = (2 * depth) ** 0.25
    groups = (x_prompt, x_sample)
    image_rows = [xg.shape[1] // _GRID_W for xg in groups for _ in range(xg.shape[0])]
    x = jnp.concatenate([xg.reshape(-1, d) for xg in groups], axis=0)
    xb = x.astype(_BF16)
    n, n_prompt = x.shape[0], x_prompt.shape[0] * x_prompt.shape[1]
    small = (attn_rpb, gmlp_ln_g, gmlp_ln_b, gmlp_bs, mix_norm_g, ln1_g, ln1_b, w_router, router_bias, ln2_g, ln2_b)
    wb = tuple(wt.astype(_BF16) for wt in (w_in, gmlp_ws, w_out, w_e_gate, w_e_up, w_e_down,
                                           w_sh_gate, w_sh_up, w_sh_down, w_ple, w_ple_gate))
    for i in range(depth):
        p = jnp.concatenate([pg[i].reshape(-1, pg.shape[-1]) for pg in (p_prompt, p_sample)], axis=0)
        last = i == depth - 1
        out_rows = [(0, n_prompt), (n_prompt, n - n_prompt)] if last else [(0, n)]
        outs = _layer(x, xb, p, image_rows, alpha, i, out_rows, tuple(wt[i] for wt in small), wb)
        x, xb = outs[0]
    return outs[0][0].reshape(x_prompt.shape), outs[1][0].reshape(x_sample.shape)
```

```python
import functools

import numpy as np
import jax
import jax.numpy as jnp
from jax import lax
from jax.experimental import pallas as pl
from jax.experimental.pallas import tpu as pltpu

_F32 = jnp.float32
_BF16 = jnp.bfloat16
_U32 = jnp.uint32

_HEAD_DIM = 128
_N_HEADS = 16
_N_GROUPS = 16
_ATTN_W = _N_HEADS * _HEAD_DIM
_GMLP_W = _N_GROUPS * _HEAD_DIM
_CHUNK = 128
_GRID_W = 64
_WIN_H = 8
_WIN_W = 16
_N_EXPERTS = 64
_TOP_K = 8
_N_EGROUPS = 8
_EGROUP = _N_EXPERTS // _N_EGROUPS
_TOPK_GROUPS = 4
_EXPERT_FF = 128
_ROUTED_SCALE = 2.5
_LN_EPS = 1e-5
_NEG_INF = -1e30
_LANES = 128
_EXPERT_TILE = 256
_GATHER_TOKENS = 128
_SLAB = 16
_PITCH = 24

_V7X_VMEM_LIMIT = 56 * 1024 * 1024


def _params(n_axes):
    return pltpu.CompilerParams(dimension_semantics=("arbitrary",) * n_axes,
                                vmem_limit_bytes=_V7X_VMEM_LIMIT)


def _mm_kernel(x_ref, w_ref, o_ref):
    o_ref[...] = jnp.dot(x_ref[...], w_ref[...], preferred_element_type=_F32).astype(o_ref.dtype)


def _matmul(x, w, layer, out_dtype, tm, tn, name):
    m, k = x.shape
    n = w.shape[2]
    tm, tn = min(tm, m), min(tn, n)
    return pl.pallas_call(
        _mm_kernel,
        out_shape=jax.ShapeDtypeStruct((m, n), out_dtype),
        grid=(m // tm, n // tn),
        in_specs=[pl.BlockSpec((tm, k), lambda i, j: (i, 0)),
                  pl.BlockSpec((None, k, tn), lambda i, j: (layer, 0, j))],
        out_specs=pl.BlockSpec((tm, tn), lambda i, j: (i, j)),
        compiler_params=_params(2),
        name=name,
    )(x, w)


def _mm2_kernel(a_ref, g_ref, wa_ref, wg_ref, o_ref):
    acc = jnp.dot(a_ref[...], wa_ref[...], preferred_element_type=_F32)
    acc += jnp.dot(g_ref[...], wg_ref[...], preferred_element_type=_F32)
    o_ref[...] = acc


def _out_proj(a, g, w_out, layer, tm, tn):
    m, ka = a.shape
    kg = g.shape[1]
    n = w_out.shape[2]
    tm, tn = min(tm, m), min(tn, n)
    return pl.pallas_call(
        _mm2_kernel,
        out_shape=jax.ShapeDtypeStruct((m, n), _F32),
        grid=(m // tm, n // tn),
        in_specs=[pl.BlockSpec((tm, ka), lambda i, j: (i, 0)),
                  pl.BlockSpec((tm, kg), lambda i, j: (i, 0)),
                  pl.BlockSpec((None, ka, tn), lambda i, j: (layer, 0, j)),
                  pl.BlockSpec((None, kg, tn), lambda i, j: (layer, 1, j))],
        out_specs=pl.BlockSpec((tm, tn), lambda i, j: (i, j)),
        compiler_params=_params(2),
        name="out_proj",
    )(a, g, w_out, w_out)


def _attn_kernel(rs_ref, d0_ref, q_ref, *rest):
    del rs_ref, d0_ref
    k_refs = rest[:_WIN_H]
    v_refs = rest[_WIN_H:2 * _WIN_H]
    bias_ref, gain_ref, o_ref, acc_ref = rest[2 * _WIN_H:]
    scale = _HEAD_DIM ** -0.5
    ss = jnp.zeros((_GRID_W, 1), _F32)
    for h in range(_N_HEADS):
        sl = slice(h * _HEAD_DIM, (h + 1) * _HEAD_DIM)
        q = q_ref[:, sl]
        k = jnp.concatenate([r[:, sl] for r in k_refs], axis=0)
        v = jnp.concatenate([r[:, sl] for r in v_refs], axis=0)
        s = lax.dot_general(q, k, (((1,), (1,)), ((), ())), preferred_element_type=_F32)
        s = s * scale + bias_ref[h]
        m = jnp.max(s, axis=-1, keepdims=True)
        p = jnp.exp(s - m)
        l = jnp.sum(p, axis=-1, keepdims=True)
        o = jnp.dot(p.astype(_BF16), v, preferred_element_type=_F32) / l
        acc_ref[:, sl] = o
        ss = ss + jnp.sum(o * o, axis=-1, keepdims=True)
    r = lax.rsqrt(ss / _ATTN_W + _LN_EPS)
    o_ref[...] = (acc_ref[...] * r * gain_ref[...]).astype(o_ref.dtype)


def _attention(proj, bias_tbl, gain_a, rs_tbl, d0_tbl):
    n = proj.shape[0]
    nr = n // _GRID_W
    proj3 = proj.reshape(nr, _GRID_W, proj.shape[1])
    blk = (None, _GRID_W, _ATTN_W)
    in_specs = [pl.BlockSpec(blk, lambda r, rs, d0: (r, 0, 0))]
    for c in (1, 2):
        for i in range(_WIN_H):
            in_specs.append(pl.BlockSpec(blk, lambda r, rs, d0, i=i, c=c: (rs[r] + i, 0, c)))
    in_specs.append(pl.BlockSpec((None, _N_HEADS, _GRID_W, _WIN_H * _GRID_W),
                                 lambda r, rs, d0: (d0[r], 0, 0, 0)))
    in_specs.append(pl.BlockSpec((1, _ATTN_W), lambda r, rs, d0: (0, 0)))
    out = pl.pallas_call(
        _attn_kernel,
        out_shape=jax.ShapeDtypeStruct((nr, _GRID_W, _ATTN_W), _BF16),
        grid_spec=pltpu.PrefetchScalarGridSpec(
            num_scalar_prefetch=2,
            grid=(nr,),
            in_specs=in_specs,
            out_specs=pl.BlockSpec(blk, lambda r, rs, d0: (r, 0, 0)),
            scratch_shapes=[pltpu.VMEM((_GRID_W, _ATTN_W), _F32)],
        ),
        compiler_params=_params(1),
        name="na2d_attention",
    )(rs_tbl, d0_tbl, *([proj3] * (1 + 2 * _WIN_H)), bias_tbl, gain_a)
    return out.reshape(n, _ATTN_W)


def _attn_tables(image_rows):
    rs_tbl, d0_tbl, base = [], [], 0
    for rows in image_rows:
        kh = min(_WIN_H, rows)
        assert kh == _WIN_H
        for r in range(rows):
            rs = min(max(r - kh // 2, 0), rows - kh)
            rs_tbl.append(base + rs)
            d0_tbl.append(rs - r + _WIN_H - 1)
        base += rows
    return np.asarray(rs_tbl, np.int32), np.asarray(d0_tbl, np.int32)


def _attn_bias_table(rpb):
    cols = np.arange(_GRID_W)
    col_start = np.clip(cols - _WIN_W // 2, 0, _GRID_W - _WIN_W)
    col_mask = (cols[None, :] >= col_start[:, None]) & (cols[None, :] < col_start[:, None] + _WIN_W)
    col_off = np.clip(cols[None, :] - cols[:, None] + (_WIN_W - 1), 0, 2 * _WIN_W - 2)
    rpb_c = jnp.where(col_mask[None, None], rpb[:, :, col_off].astype(_F32), _F32(_NEG_INF))
    tabs = []
    for d0 in range(_WIN_H):
        t = rpb_c[:, d0:d0 + _WIN_H].transpose(0, 2, 1, 3)
        tabs.append(t.reshape(_N_HEADS, _GRID_W, _WIN_H * _GRID_W))
    return jnp.stack(tabs)


def _gmlp_kernel(u_ref, v_ref, ws_ref, lng_ref, lnb_ref, bsx_ref, gain_ref, o_ref, acc_ref):
    ss = jnp.zeros((_CHUNK, 1), _F32)
    for g in range(_N_GROUPS):
        sl = slice(g * _HEAD_DIM, (g + 1) * _HEAD_DIM)
        v = jax.nn.gelu(v_ref[:, sl].astype(_F32))
        mu = jnp.mean(v, axis=-1, keepdims=True)
        c = v - mu
        var = jnp.mean(c * c, axis=-1, keepdims=True)
        vn = c * lax.rsqrt(var + _LN_EPS) * lng_ref[:, sl] + lnb_ref[:, sl]
        mixed = jnp.dot(ws_ref[g], vn.astype(_BF16), preferred_element_type=_F32) + bsx_ref[:, sl]
        o = jax.nn.gelu(u_ref[:, sl].astype(_F32)) * mixed
        acc_ref[:, sl] = o
        ss = ss + jnp.sum(o * o, axis=-1, keepdims=True)
    r = lax.rsqrt(ss / _GMLP_W + _LN_EPS)
    o_ref[...] = (acc_ref[...] * r * gain_ref[...]).astype(o_ref.dtype)


def _gmlp(proj, ws, ln_g, ln_b, bsx, gain_g):
    n = proj.shape[0]
    row = pl.BlockSpec((1, _GMLP_W), lambda i: (0, 0))
    return pl.pallas_call(
        _gmlp_kernel,
        out_shape=jax.ShapeDtypeStruct((n, _GMLP_W), _BF16),
        grid=(n // _CHUNK,),
        in_specs=[pl.BlockSpec((_CHUNK, _GMLP_W), lambda i: (i, 3)),
                  pl.BlockSpec((_CHUNK, _GMLP_W), lambda i: (i, 4)),
                  pl.BlockSpec((_N_GROUPS, _CHUNK, _CHUNK), lambda i: (0, 0, 0)),
                  row, row,
                  pl.BlockSpec((_CHUNK, _GMLP_W), lambda i: (0, 0)),
                  row],
        out_specs=pl.BlockSpec((_CHUNK, _GMLP_W), lambda i: (i, 0)),
        scratch_shapes=[pltpu.VMEM((_CHUNK, _GMLP_W), _F32)],
        compiler_params=_params(1),
        name="gmlp_gating",
    )(proj, proj, ws, ln_g, ln_b, bsx, gain_g)


def _layer_norm(z, g, b):
    mu = jnp.mean(z, axis=-1, keepdims=True)
    c = z - mu
    var = jnp.mean(c * c, axis=-1, keepdims=True)
    return c * lax.rsqrt(var + _LN_EPS) * g + b


def _pack_halves(x):
    k = x.shape[1] // 2
    bits = lambda v: lax.bitcast_convert_type(v.astype(_BF16).astype(_F32), _U32)
    return (bits(x[:, k:]) & _U32(0xFFFF0000)) | (bits(x[:, :k]) >> _U32(16))


def _unpack_halves(words):
    lo = lax.bitcast_convert_type(words << _U32(16), _F32)
    hi = lax.bitcast_convert_type(words & _U32(0xFFFF0000), _F32)
    return lo, hi


def _store_slabs(ref, words):
    m = words.shape[0]
    for s in range(_PITCH):
        piece = words[:, s * _LANES:(s + 1) * _LANES] if s < _SLAB else jnp.zeros((m, _LANES), words.dtype)
        ref[pl.ds(s, m, stride=_PITCH), :] = piece


def _load_slabs(ref, first, m):
    return jnp.concatenate([ref[pl.ds(first * _PITCH + s, m, stride=_PITCH), :] for s in range(_SLAB)], axis=1)


def _slab_copy(src_ref, src_slab, dst_ref, dst_slab, sem, rows):
    src = src_ref.at[pl.ds(pl.multiple_of(src_slab * _PITCH, 8), rows)]
    dst = dst_ref.at[pl.ds(pl.multiple_of(dst_slab * _PITCH, 8), rows)]
    return pltpu.make_async_copy(src, dst, sem)


def _wait_slabs(src_ref, dst_ref, n_slabs, sem, rows):
    pltpu.make_async_copy(src_ref.at[pl.ds(0, n_slabs * rows)], dst_ref.at[pl.ds(0, n_slabs * rows)], sem).wait()


def _add_ln_kernel(alpha, res_ref, y_ref, g_ref, b_ref, o32_ref, o16_ref, opk_ref):
    o = _layer_norm(alpha * res_ref[...] + y_ref[...], g_ref[...], b_ref[...])
    o32_ref[...] = o
    o16_ref[...] = o.astype(_BF16)
    _store_slabs(opk_ref, _pack_halves(o))


def _add_ln(res, y, g, b, alpha, tm):
    m, d = res.shape
    tm = min(tm, m)
    tile = pl.BlockSpec((tm, d), lambda i: (i, 0))
    half = pl.BlockSpec((tm * _PITCH, _LANES), lambda i: (i, 0))
    row = pl.BlockSpec((1, d), lambda i: (0, 0))
    return pl.pallas_call(
        functools.partial(_add_ln_kernel, alpha),
        out_shape=(jax.ShapeDtypeStruct((m, d), _F32), jax.ShapeDtypeStruct((m, d), _BF16),
                   jax.ShapeDtypeStruct((m * _PITCH, _LANES), _U32)),
        grid=(m // tm,),
        in_specs=[tile, tile, row, row],
        out_specs=(tile, tile, half),
        compiler_params=_params(1),
        name="residual_layernorm",
    )(res, y, g, b)


def _first_argmax(x, iota, n):
    m = jnp.max(x, axis=0, keepdims=True)
    return jnp.min(jnp.where(x == m, iota, n), axis=0, keepdims=True)


def _router_kernel(h_ref, wr_ref, bias_ref, tri_ref, e_ref, w_ref, r_ref, cnt_ref, carry_ref):
    tm = h_ref.shape[0]

    @pl.when(pl.program_id(0) == 0)
    def _():
        carry_ref[...] = jnp.zeros_like(carry_ref)

    logits = jnp.dot(h_ref[...], wr_ref[...], precision=lax.Precision.HIGHEST,
                     preferred_element_type=_F32)
    scores = jax.nn.sigmoid(logits.T[:_N_EXPERTS])
    bias = jnp.concatenate([bias_ref[...]] * (tm // _LANES), axis=1)
    sel = scores + bias
    ninf = _F32(-jnp.inf)

    iota_g = lax.broadcasted_iota(jnp.int32, (_EGROUP, tm), 0)
    grp = []
    for g in range(_N_EGROUPS):
        x = sel[g * _EGROUP:(g + 1) * _EGROUP]
        m1 = jnp.max(x, axis=0, keepdims=True)
        hit = iota_g == _first_argmax(x, iota_g, _EGROUP)
        m2 = jnp.max(jnp.where(hit, ninf, x), axis=0, keepdims=True)
        grp.append(m1 + m2)
    grp = jnp.concatenate(grp, axis=0)

    iota_n = lax.broadcasted_iota(jnp.int32, (_N_EGROUPS, tm), 0)
    gmask = jnp.zeros((_N_EGROUPS, tm), _F32)
    for _ in range(_TOPK_GROUPS):
        hit = iota_n == _first_argmax(grp, iota_n, _N_EGROUPS)
        gmask = jnp.where(hit, 1.0, gmask)
        grp = jnp.where(hit, ninf, grp)

    masked = jnp.concatenate(
        [jnp.where(gmask[g:g + 1] > 0, sel[g * _EGROUP:(g + 1) * _EGROUP], ninf) for g in range(_N_EGROUPS)], axis=0)
    iota_e = lax.broadcasted_iota(jnp.int32, (_N_EXPERTS, tm), 0)
    chosen = jnp.zeros((_N_EXPERTS, tm), _F32)
    picks = []
    for _ in range(_TOP_K):
        idx = _first_argmax(masked, iota_e, _N_EXPERTS)
        hit = iota_e == idx
        chosen = jnp.where(hit, 1.0, chosen)
        masked = jnp.where(hit, ninf, masked)
        picks.append(idx)

    w = scores * chosen
    gates = w / jnp.sum(w, axis=0, keepdims=True) * _ROUTED_SCALE

    carry = carry_ref[...]
    rank = carry[:, 0:1] + jnp.dot(chosen.astype(_BF16), tri_ref[...], preferred_element_type=_F32)
    carry = carry + jnp.sum(chosen, axis=1, keepdims=True)
    carry_ref[...] = carry
    cnt_ref[...] = carry

    pick_w, pick_r = [], []
    for idx in picks:
        hit = iota_e == idx
        pick_w.append(jnp.sum(jnp.where(hit, gates, 0.0), axis=0, keepdims=True))
        pick_r.append(jnp.sum(jnp.where(hit, rank, 0.0), axis=0, keepdims=True))
    e_ref[...] = jnp.concatenate(picks, axis=0)
    w_ref[...] = jnp.concatenate(pick_w, axis=0)
    r_ref[...] = jnp.concatenate(pick_r, axis=0).astype(jnp.int32)


def _router(h, w_router_p, bias_p, tm):
    m, d = h.shape
    tm = min(tm, m)
    tri = jnp.asarray(np.triu(np.ones((tm, tm), np.float32), 1), _BF16)
    slot = pl.BlockSpec((_TOP_K, tm), lambda i: (0, i))
    return pl.pallas_call(
        _router_kernel,
        out_shape=(jax.ShapeDtypeStruct((_TOP_K, m), jnp.int32), jax.ShapeDtypeStruct((_TOP_K, m), _F32),
                   jax.ShapeDtypeStruct((_TOP_K, m), jnp.int32), jax.ShapeDtypeStruct((_N_EXPERTS, _LANES), _F32)),
        grid=(m // tm,),
        in_specs=[pl.BlockSpec((tm, d), lambda i: (i, 0)),
                  pl.BlockSpec((d, _LANES), lambda i: (0, 0)),
                  pl.BlockSpec((_N_EXPERTS, _LANES), lambda i: (0, 0)),
                  pl.BlockSpec((tm, tm), lambda i: (0, 0))],
        out_specs=(slot, slot, slot, pl.BlockSpec((_N_EXPERTS, _LANES), lambda i: (0, 0))),
        scratch_shapes=[pltpu.VMEM((_N_EXPERTS, _LANES), _F32)],
        compiler_params=_params(1),
        name="moe_router",
    )(h, w_router_p, bias_p, tri)


def _pad_chunks():
    return [1 << b for b in range(_EXPERT_TILE.bit_length() - 1)]


def _dispatch_kernel(pad_start_ref, pad_len_ref, pos_ref, x_ref, xs_ref, zero_ref, sem, zsem):
    tm = x_ref.shape[0] // _PITCH

    def start(t, c):
        for k in range(_TOP_K):
            _slab_copy(x_ref, t, xs_ref, pos_ref[t * _TOP_K + k], sem, _PITCH).start(priority=k % 2)
        return c

    lax.fori_loop(0, tm, start, 0)

    @pl.when(pl.program_id(0) == 0)
    def _():
        zero_ref[...] = jnp.zeros_like(zero_ref)

        def fill(e, do_start):
            cur = pad_start_ref[e]
            n = pad_len_ref[e]
            for c in _pad_chunks():
                @pl.when((n & c) != 0)
                def _(cur=cur, c=c):
                    dst = xs_ref.at[pl.ds(pl.multiple_of(cur * _PITCH, 8), c * _PITCH)]
                    cp = pltpu.make_async_copy(zero_ref.at[pl.ds(0, c * _PITCH)], dst, zsem)
                    if do_start:
                        cp.start()
                    else:
                        cp.wait()
                cur = cur + (n & c)

        n_pads = pad_start_ref.shape[0]
        lax.fori_loop(0, n_pads, lambda e, c: (fill(e, True), c)[1], 0)
        lax.fori_loop(0, n_pads, lambda e, c: (fill(e, False), c)[1], 0)

    for _ in range(_TOP_K):
        _wait_slabs(x_ref, xs_ref, tm, sem, _PITCH)


def _dispatch(xpk, pos_flat, pad_start, pad_len, n_rows):
    tm = _GATHER_TOKENS
    return pl.pallas_call(
        _dispatch_kernel,
        out_shape=jax.ShapeDtypeStruct((n_rows * _PITCH, _LANES), _U32),
        grid_spec=pltpu.PrefetchScalarGridSpec(
            num_scalar_prefetch=2,
            grid=(xpk.shape[0] // (tm * _PITCH),),
            in_specs=[pl.BlockSpec((tm * _TOP_K,), lambda i, ps, pn: (i,), memory_space=pltpu.SMEM),
                      pl.BlockSpec((tm * _PITCH, _LANES), lambda i, ps, pn: (i, 0))],
            out_specs=pl.BlockSpec(memory_space=pl.ANY),
            scratch_shapes=[pltpu.VMEM((_EXPERT_TILE // 2 * _PITCH, _LANES), _U32),
                            pltpu.SemaphoreType.DMA(()), pltpu.SemaphoreType.DMA(())],
        ),
        compiler_params=_params(1),
        name="moe_dispatch",
    )(pad_start, pad_len, pos_flat, xpk)


def _experts_kernel(te_ref, nu_ref, xs_ref, wg_ref, wu_ref, wd_ref, ys_ref):
    @pl.when(pl.program_id(0) >= nu_ref[0])
    def _():
        ys_ref[...] = jnp.zeros_like(ys_ref)

    @pl.when(pl.program_id(0) < nu_ref[0])
    def _():
        lo, hi = _unpack_halves(_load_slabs(xs_ref, 0, _EXPERT_TILE))
        k = lo.shape[1]
        w_lo = jnp.concatenate([wg_ref[:k], wu_ref[:k]], axis=1)
        w_hi = jnp.concatenate([wg_ref[k:], wu_ref[k:]], axis=1)
        gu = jnp.dot(lo.astype(_BF16), w_lo, preferred_element_type=_F32)
        gu += jnp.dot(hi.astype(_BF16), w_hi, preferred_element_type=_F32)
        act = jax.nn.silu(gu[:, :_EXPERT_FF]) * gu[:, _EXPERT_FF:]
        y = jnp.dot(act.astype(_BF16), wd_ref[...], preferred_element_type=_F32)
        _store_slabs(ys_ref, _pack_halves(y))


def _experts(xs, wg, wu, wd, layer, tile_expert, n_used):
    n_rows = xs.shape[0] // _PITCH
    d = wd.shape[3]
    w_in = pl.BlockSpec((None, None, d, _EXPERT_FF), lambda t, te, nu: (layer, te[t], 0, 0))
    tile = lambda t, te, nu: (t, 0)
    return pl.pallas_call(
        _experts_kernel,
        out_shape=jax.ShapeDtypeStruct(xs.shape, _U32),
        grid_spec=pltpu.PrefetchScalarGridSpec(
            num_scalar_prefetch=2,
            grid=(n_rows // _EXPERT_TILE,),
            in_specs=[pl.BlockSpec((_EXPERT_TILE * _PITCH, _LANES), tile),
                      w_in, w_in,
                      pl.BlockSpec((None, None, _EXPERT_FF, d), lambda t, te, nu: (layer, te[t], 0, 0))],
            out_specs=pl.BlockSpec((_EXPERT_TILE * _PITCH, _LANES), tile),
        ),
        compiler_params=_params(1),
        name="moe_experts",
    )(tile_expert, n_used, xs, wg, wu, wd)


def _mlp_kernel(x_ref, wg_ref, wu_ref, wd_ref, o_ref):
    j = pl.program_id(1)
    x = x_ref[...]
    hg = jnp.dot(x, wg_ref[...], preferred_element_type=_F32)
    hu = jnp.dot(x, wu_ref[...], preferred_element_type=_F32)
    contrib = jnp.dot((jax.nn.silu(hg) * hu).astype(_BF16), wd_ref[...], preferred_element_type=_F32)

    @pl.when(j == 0)
    def _():
        o_ref[...] = contrib

    @pl.when(j > 0)
    def _():
        o_ref[...] += contrib


def _mlp(xb, wg, wu, wd, layer, tm, tf):
    m, d = xb.shape
    ff = wg.shape[2]
    tm, tf = min(tm, m), min(tf, ff)
    return pl.pallas_call(
        _mlp_kernel,
        out_shape=jax.ShapeDtypeStruct((m, d), _F32),
        grid=(m // tm, ff // tf),
        in_specs=[pl.BlockSpec((tm, d), lambda i, j: (i, 0)),
                  pl.BlockSpec((None, d, tf), lambda i, j: (layer, 0, j)),
                  pl.BlockSpec((None, d, tf), lambda i, j: (layer, 0, j)),
                  pl.BlockSpec((None, tf, d), lambda i, j: (layer, j, 0))],
        out_specs=pl.BlockSpec((tm, d), lambda i, j: (i, 0)),
        compiler_params=_params(2),
        name="shared_expert",
    )(xb, wg, wu, wd)


def _combine_kernel(alpha, pos_ref, pos_next_ref, w_ref, ys_ref, sh_ref, h_ref, g_ref, b_ref,
                    o32_ref, o16_ref, buf_ref, sem):
    tm = h_ref.shape[0]
    i = pl.program_id(0)
    slot = i % 2

    def gather(p_ref, into):
        def start(t, c):
            for k in range(_TOP_K):
                dst = (into * _TOP_K + k) * tm + t
                _slab_copy(ys_ref, p_ref[t * _TOP_K + k], buf_ref, dst, sem.at[into], _SLAB).start(priority=k % 2)
            return c
        lax.fori_loop(0, tm, start, 0)

    @pl.when(i == 0)
    def _():
        gather(pos_ref, 0)

    @pl.when(i + 1 < pl.num_programs(0))
    def _():
        gather(pos_next_ref, 1 - slot)

    _wait_slabs(ys_ref, buf_ref, _TOP_K * tm, sem.at[slot], _SLAB)

    acc_lo = acc_hi = None
    for k in range(_TOP_K):
        lo, hi = _unpack_halves(_load_slabs(buf_ref, (slot * _TOP_K + k) * tm, tm))
        wk = w_ref[:, k:k + 1]
        acc_lo = lo * wk if acc_lo is None else acc_lo + lo * wk
        acc_hi = hi * wk if acc_hi is None else acc_hi + hi * wk
    moe = jnp.concatenate([acc_lo, acc_hi], axis=1) + sh_ref[...]
    o = _layer_norm(alpha * h_ref[...] + moe, g_ref[...], b_ref[...])
    o32_ref[...] = o
    o16_ref[...] = o.astype(_BF16)


def _combine(ys, pos_flat, w_tok, shared, h, g, b, alpha):
    m, d = h.shape
    tm = _GATHER_TOKENS
    tile = pl.BlockSpec((tm, d), lambda i: (i, 0))
    row = pl.BlockSpec((1, d), lambda i: (0, 0))
    return pl.pallas_call(
        functools.partial(_combine_kernel, alpha),
        out_shape=(jax.ShapeDtypeStruct((m, d), _F32), jax.ShapeDtypeStruct((m, d), _BF16)),
        grid=(m // tm,),
        in_specs=[pl.BlockSpec((tm * _TOP_K,), lambda i: (i,), memory_space=pltpu.SMEM),
                  pl.BlockSpec((tm * _TOP_K,), lambda i: (jnp.minimum(i + 1, m // tm - 1),), memory_space=pltpu.SMEM),
                  pl.BlockSpec((tm, _TOP_K), lambda i: (i, 0)),
                  pl.BlockSpec(memory_space=pl.ANY),
                  tile, tile, row, row],
        out_specs=(tile, tile),
        scratch_shapes=[pltpu.VMEM((2 * _TOP_K * tm * _PITCH, _LANES), _U32), pltpu.SemaphoreType.DMA((2,))],
        compiler_params=_params(1),
        name="moe_combine_layernorm",
    )(pos_flat, pos_flat, w_tok, ys, shared, h, g, b)


def _routing_plan(e_t, r_t, counts, n_tiles):
    counts = counts.astype(jnp.int32)
    padded = (counts + _EXPERT_TILE - 1) // _EXPERT_TILE * _EXPERT_TILE
    ends = jnp.cumsum(padded)
    offsets = ends - padded
    experts = jnp.arange(_N_EXPERTS, dtype=jnp.int32)
    pos = r_t + jnp.sum(jnp.where(e_t[..., None] == experts, offsets, 0), axis=-1)
    tile_start = jnp.arange(n_tiles, dtype=jnp.int32) * _EXPERT_TILE
    tile_expert = jnp.sum((ends[None, :] <= tile_start[:, None]).astype(jnp.int32), axis=1)
    tile_expert = jnp.minimum(tile_expert, _N_EXPERTS - 1)
    n_used = (ends[-1:] // _EXPERT_TILE).astype(jnp.int32)
    half = _EXPERT_TILE // 2
    tail_start = ends[-1] + jnp.arange(2 * _N_EXPERTS, dtype=jnp.int32) * half
    tail_len = jnp.where(tail_start < n_tiles * _EXPERT_TILE, half, 0)
    pad_start = jnp.concatenate([offsets + counts, tail_start]).astype(jnp.int32)
    pad_len = jnp.concatenate([padded - counts, tail_len]).astype(jnp.int32)
    return pos.T.reshape(-1), tile_expert, n_used, pad_start, pad_len


def _ple_kernel(hb_ref, h_ref, p_ref, wpg_ref, wple_ref, o32_ref, o16_ref):
    gate = jax.nn.sigmoid(jnp.dot(hb_ref[...], wpg_ref[...], preferred_element_type=_F32))
    pe = jnp.dot(p_ref[...].astype(_BF16), wple_ref[...], preferred_element_type=_F32)
    o = h_ref[...] + gate * pe
    o32_ref[...] = o
    o16_ref[...] = o.astype(_BF16)


def _ple(hb, h, p, wpg, wple, layer, row0, rows, tm, tn):
    d = hb.shape[1]
    pd = p.shape[1]
    tm, tn = min(tm, rows), min(tn, d)
    i0 = row0 // tm
    assert row0 % tm == 0 and rows % tm == 0
    out_tile = pl.BlockSpec((tm, tn), lambda i, j: (i, j))
    return pl.pallas_call(
        _ple_kernel,
        out_shape=(jax.ShapeDtypeStruct((rows, d), _F32), jax.ShapeDtypeStruct((rows, d), _BF16)),
        grid=(rows // tm, d // tn),
        in_specs=[pl.BlockSpec((tm, d), lambda i, j: (i0 + i, 0)),
                  pl.BlockSpec((tm, tn), lambda i, j: (i0 + i, j)),
                  pl.BlockSpec((tm, pd), lambda i, j: (i0 + i, 0)),
                  pl.BlockSpec((None, d, tn), lambda i, j: (layer, 0, j)),
                  pl.BlockSpec((None, pd, tn), lambda i, j: (layer, 0, j))],
        out_specs=(out_tile, out_tile),
        compiler_params=_params(2),
        name="ple_gate",
    )(hb, h, p, wpg, wple)


def _layer(x, xb, p, image_rows, alpha, layer, out_rows, w, wb):
    (attn_rpb, gmlp_ln_g, gmlp_ln_b, gmlp_bs, mix_norm_g, ln1_g, ln1_b, w_router, router_bias, ln2_g, ln2_b) = w
    (w_in, gmlp_ws, w_out, w_e_gate, w_e_up, w_e_down, w_sh_gate, w_sh_up, w_sh_down, w_ple, w_ple_gate) = wb
    row = lambda v: v.reshape(1, -1).astype(_F32)

    proj = _matmul(xb, w_in, layer, _BF16, 1024, 1024, "in_proj")

    rs_tbl, d0_tbl = _attn_tables(image_rows)
    a = _attention(proj, _attn_bias_table(attn_rpb), row(mix_norm_g[:_ATTN_W]),
                   jnp.asarray(rs_tbl), jnp.asarray(d0_tbl))

    bsx = jnp.repeat(gmlp_bs.T.astype(_F32), _HEAD_DIM, axis=1)
    g = _gmlp(proj, gmlp_ws[layer], row(gmlp_ln_g), row(gmlp_ln_b), bsx, row(mix_norm_g[_ATTN_W:]))

    mixed_out = _out_proj(a, g, w_out, layer, 1024, 1024)
    h, hb, hpk = _add_ln(x, mixed_out, row(ln1_g), row(ln1_b), alpha, 256)

    wr_p = jnp.pad(w_router.astype(_F32), ((0, 0), (0, _LANES - _N_EXPERTS)))
    bias_p = jnp.broadcast_to(router_bias.astype(_F32)[:, None], (_N_EXPERTS, _LANES))
    e_t, w_t, r_t, counts = _router(h, wr_p, bias_p, 512)

    n_rows = x.shape[0] * _TOP_K + _N_EXPERTS * _EXPERT_TILE
    pos_flat, tile_expert, n_used, pad_start, pad_len = _routing_plan(
        e_t, r_t, counts[:, 0], n_rows // _EXPERT_TILE)
    xs = _dispatch(hpk, pos_flat, pad_start, pad_len, n_rows)
    ys = _experts(xs, w_e_gate, w_e_up, w_e_down, layer, tile_expert, n_used)
    shared = _mlp(hb, w_sh_gate, w_sh_up, w_sh_down, layer, 512, 512)
    h2, h2b = _combine(ys, pos_flat, w_t.T, shared, h, row(ln2_g), row(ln2_b), alpha)
    return [_ple(h2b, h2, p, w_ple_gate, w_ple, layer, r0, nr, 1024, 512) for r0, nr in out_rows]


def kernel(x_prompt, x_sample, p_prompt, p_sample, w_in, attn_rpb, gmlp_ln_g, gmlp_ln_b, gmlp_ws, gmlp_bs,
           mix_norm_g, w_out, ln1_g, ln1_b, w_router, router_bias, w_e_gate, w_e_up, w_e_down,
           w_sh_gate, w_sh_up, w_sh_down, ln2_g, ln2_b, w_ple, w_ple_gate):
    depth = w_in.shape[0]
    d = x_prompt.shape[-1]
    alpha = (2 * depth) ** 0.25
    groups = (x_prompt, x_sample)
    image_rows = [xg.shape[1] // _GRID_W for xg in groups for _ in range(xg.shape[0])]
    x = jnp.concatenate([xg.reshape(-1, d) for xg in groups], axis=0)
    xb = x.astype(_BF16)
    n, n_prompt = x.shape[0], x_prompt.shape[0] * x_prompt.shape[1]
    small = (attn_rpb, gmlp_ln_g, gmlp_ln_b, gmlp_bs, mix_norm_g, ln1_g, ln1_b, w_router, router_bias, ln2_g, ln2_b)
    wb = tuple(wt.astype(_BF16) for wt in (w_in, gmlp_ws, w_out, w_e_gate, w_e_up, w_e_down,
                                           w_sh_gate, w_sh_up, w_sh_down, w_ple, w_ple_gate))
    for i in range(depth):
        p = jnp.concatenate([pg[i].reshape(-1, pg.shape[-1]) for pg in (p_prompt, p_sample)], axis=0)
        last = i == depth - 1
        out_rows = [(0, n_prompt), (n_prompt, n - n_prompt)] if last else [(0, n)]
        outs = _layer(x, xb, p, image_rows, alpha, i, out_rows, tuple(wt[i] for wt in small), wb)
        x, xb = outs[0]
    return outs[0][0].reshape(x_prompt.shape), outs[1][0].reshape(x_sample.shape)
```

```python
import functools

import numpy as np
import jax
import jax.numpy as jnp
from jax import lax
from jax.experimental import pallas as pl
from jax.experimental.pallas import tpu as pltpu

_F32 = jnp.float32
_BF16 = jnp.bfloat16
_U32 = jnp.uint32

_HEAD_DIM = 128
_N_HEADS = 16
_N_GROUPS = 16
_ATTN_W = _N_HEADS * _HEAD_DIM
_GMLP_W = _N_GROUPS * _HEAD_DIM
_CHUNK = 128
_GRID_W = 64
_WIN_H = 8
_WIN_W = 16
_N_EXPERTS = 64
_TOP_K = 8
_N_EGROUPS = 8
_EGROUP = _N_EXPERTS // _N_EGROUPS
_TOPK_GROUPS = 4
_EXPERT_FF = 128
_ROUTED_SCALE = 2.5
_LN_EPS = 1e-5
_NEG_INF = -1e30
_LANES = 128
_EXPERT_TILE = 512
_GATHER_TOKENS = 128
_SLAB = 16
_PITCH = 24

_V7X_VMEM_LIMIT = 56 * 1024 * 1024


def _params(n_axes):
    return pltpu.CompilerParams(dimension_semantics=("arbitrary",) * n_axes,
                                vmem_limit_bytes=_V7X_VMEM_LIMIT)


def _mm_kernel(x_ref, w_ref, o_ref):
    o_ref[...] = jnp.dot(x_ref[...], w_ref[...], preferred_element_type=_F32).astype(o_ref.dtype)


def _matmul(x, w, layer, out_dtype, tm, tn, name):
    m, k = x.shape
    n = w.shape[2]
    tm, tn = min(tm, m), min(tn, n)
    return pl.pallas_call(
        _mm_kernel,
        out_shape=jax.ShapeDtypeStruct((m, n), out_dtype),
        grid=(m // tm, n // tn),
        in_specs=[pl.BlockSpec((tm, k), lambda i, j: (i, 0)),
                  pl.BlockSpec((None, k, tn), lambda i, j: (layer, 0, j))],
        out_specs=pl.BlockSpec((tm, tn), lambda i, j: (i, j)),
        compiler_params=_params(2),
        name=name,
    )(x, w)


def _mm2_kernel(a_ref, g_ref, wa_ref, wg_ref, o_ref):
    acc = jnp.dot(a_ref[...], wa_ref[...], preferred_element_type=_F32)
    acc += jnp.dot(g_ref[...], wg_ref[...], preferred_element_type=_F32)
    o_ref[...] = acc


def _out_proj(a, g, w_out, layer, tm, tn):
    m, ka = a.shape
    kg = g.shape[1]
    n = w_out.shape[2]
    tm, tn = min(tm, m), min(tn, n)
    return pl.pallas_call(
        _mm2_kernel,
        out_shape=jax.ShapeDtypeStruct((m, n), _F32),
        grid=(m // tm, n // tn),
        in_specs=[pl.BlockSpec((tm, ka), lambda i, j: (i, 0)),
                  pl.BlockSpec((tm, kg), lambda i, j: (i, 0)),
                  pl.BlockSpec((None, ka, tn), lambda i, j: (layer, 0, j)),
                  pl.BlockSpec((None, kg, tn), lambda i, j: (layer, 1, j))],
        out_specs=pl.BlockSpec((tm, tn), lambda i, j: (i, j)),
        compiler_params=_params(2),
        name="out_proj",
    )(a, g, w_out, w_out)


def _attn_kernel(rs_ref, d0_ref, q_ref, *rest):
    del rs_ref, d0_ref
    k_refs = rest[:_WIN_H]
    v_refs = rest[_WIN_H:2 * _WIN_H]
    bias_ref, gain_ref, o_ref, acc_ref = rest[2 * _WIN_H:]
    scale = _HEAD_DIM ** -0.5
    ss = jnp.zeros((_GRID_W, 1), _F32)
    for h in range(_N_HEADS):
        sl = slice(h * _HEAD_DIM, (h + 1) * _HEAD_DIM)
        q = q_ref[:, sl]
        k = jnp.concatenate([r[:, sl] for r in k_refs], axis=0)
        v = jnp.concatenate([r[:, sl] for r in v_refs], axis=0)
        s = lax.dot_general(q, k, (((1,), (1,)), ((), ())), preferred_element_type=_F32)
        s = s * scale + bias_ref[h]
        m = jnp.max(s, axis=-1, keepdims=True)
        p = jnp.exp(s - m)
        l = jnp.sum(p, axis=-1, keepdims=True)
        o = jnp.dot(p.astype(_BF16), v, preferred_element_type=_F32) / l
        acc_ref[:, sl] = o
        ss = ss + jnp.sum(o * o, axis=-1, keepdims=True)
    r = lax.rsqrt(ss / _ATTN_W + _LN_EPS)
    o_ref[...] = (acc_ref[...] * r * gain_ref[...]).astype(o_ref.dtype)


def _attention(proj, bias_tbl, gain_a, rs_tbl, d0_tbl):
    n = proj.shape[0]
    nr = n // _GRID_W
    proj3 = proj.reshape(nr, _GRID_W, proj.shape[1])
    blk = (None, _GRID_W, _ATTN_W)
    in_specs = [pl.BlockSpec(blk, lambda r, rs, d0: (r, 0, 0))]
    for c in (1, 2):
        for i in range(_WIN_H):
            in_specs.append(pl.BlockSpec(blk, lambda r, rs, d0, i=i, c=c: (rs[r] + i, 0, c)))
    in_specs.append(pl.BlockSpec((None, _N_HEADS, _GRID_W, _WIN_H * _GRID_W),
                                 lambda r, rs, d0: (d0[r], 0, 0, 0)))
    in_specs.append(pl.BlockSpec((1, _ATTN_W), lambda r, rs, d0: (0, 0)))
    out = pl.pallas_call(
        _attn_kernel,
        out_shape=jax.ShapeDtypeStruct((nr, _GRID_W, _ATTN_W), _BF16),
        grid_spec=pltpu.PrefetchScalarGridSpec(
            num_scalar_prefetch=2,
            grid=(nr,),
            in_specs=in_specs,
            out_specs=pl.BlockSpec(blk, lambda r, rs, d0: (r, 0, 0)),
            scratch_shapes=[pltpu.VMEM((_GRID_W, _ATTN_W), _F32)],
        ),
        compiler_params=_params(1),
        name="na2d_attention",
    )(rs_tbl, d0_tbl, *([proj3] * (1 + 2 * _WIN_H)), bias_tbl, gain_a)
    return out.reshape(n, _ATTN_W)


def _attn_tables(image_rows):
    rs_tbl, d0_tbl, base = [], [], 0
    for rows in image_rows:
        kh = min(_WIN_H, rows)
        assert kh == _WIN_H
        for r in range(rows):
            rs = min(max(r - kh // 2, 0), rows - kh)
            rs_tbl.append(base + rs)
            d0_tbl.append(rs - r + _WIN_H - 1)
        base += rows
    return np.asarray(rs_tbl, np.int32), np.asarray(d0_tbl, np.int32)


def _attn_bias_table(rpb):
    cols = np.arange(_GRID_W)
    col_start = np.clip(cols - _WIN_W // 2, 0, _GRID_W - _WIN_W)
    col_mask = (cols[None, :] >= col_start[:, None]) & (cols[None, :] < col_start[:, None] + _WIN_W)
    col_off = np.clip(cols[None, :] - cols[:, None] + (_WIN_W - 1), 0, 2 * _WIN_W - 2)
    rpb_c = jnp.where(col_mask[None, None], rpb[:, :, col_off].astype(_F32), _F32(_NEG_INF))
    tabs = []
    for d0 in range(_WIN_H):
        t = rpb_c[:, d0:d0 + _WIN_H].transpose(0, 2, 1, 3)
        tabs.append(t.reshape(_N_HEADS, _GRID_W, _WIN_H * _GRID_W))
    return jnp.stack(tabs)


def _gmlp_kernel(u_ref, v_ref, ws_ref, lng_ref, lnb_ref, bsx_ref, gain_ref, o_ref, acc_ref):
    ss = jnp.zeros((_CHUNK, 1), _F32)
    for g in range(_N_GROUPS):
        sl = slice(g * _HEAD_DIM, (g + 1) * _HEAD_DIM)
        v = jax.nn.gelu(v_ref[:, sl].astype(_F32))
        mu = jnp.mean(v, axis=-1, keepdims=True)
        c = v - mu
        var = jnp.mean(c * c, axis=-1, keepdims=True)
        vn = c * lax.rsqrt(var + _LN_EPS) * lng_ref[:, sl] + lnb_ref[:, sl]
        mixed = jnp.dot(ws_ref[g], vn.astype(_BF16), preferred_element_type=_F32) + bsx_ref[:, sl]
        o = jax.nn.gelu(u_ref[:, sl].astype(_F32)) * mixed
        acc_ref[:, sl] = o
        ss = ss + jnp.sum(o * o, axis=-1, keepdims=True)
    r = lax.rsqrt(ss / _GMLP_W + _LN_EPS)
    o_ref[...] = (acc_ref[...] * r * gain_ref[...]).astype(o_ref.dtype)


def _gmlp(proj, ws, ln_g, ln_b, bsx, gain_g):
    n = proj.shape[0]
    row = pl.BlockSpec((1, _GMLP_W), lambda i: (0, 0))
    return pl.pallas_call(
        _gmlp_kernel,
        out_shape=jax.ShapeDtypeStruct((n, _GMLP_W), _BF16),
        grid=(n // _CHUNK,),
        in_specs=[pl.BlockSpec((_CHUNK, _GMLP_W), lambda i: (i, 3)),
                  pl.BlockSpec((_CHUNK, _GMLP_W), lambda i: (i, 4)),
                  pl.BlockSpec((_N_GROUPS, _CHUNK, _CHUNK), lambda i: (0, 0, 0)),
                  row, row,
                  pl.BlockSpec((_CHUNK, _GMLP_W), lambda i: (0, 0)),
                  row],
        out_specs=pl.BlockSpec((_CHUNK, _GMLP_W), lambda i: (i, 0)),
        scratch_shapes=[pltpu.VMEM((_CHUNK, _GMLP_W), _F32)],
        compiler_params=_params(1),
        name="gmlp_gating",
    )(proj, proj, ws, ln_g, ln_b, bsx, gain_g)


def _layer_norm(z, g, b):
    mu = jnp.mean(z, axis=-1, keepdims=True)
    c = z - mu
    var = jnp.mean(c * c, axis=-1, keepdims=True)
    return c * lax.rsqrt(var + _LN_EPS) * g + b


def _pack_halves(x):
    k = x.shape[1] // 2
    bits = lambda v: lax.bitcast_convert_type(v.astype(_BF16).astype(_F32), _U32)
    return (bits(x[:, k:]) & _U32(0xFFFF0000)) | (bits(x[:, :k]) >> _U32(16))


def _unpack_halves(words):
    lo = lax.bitcast_convert_type(words << _U32(16), _F32)
    hi = lax.bitcast_convert_type(words & _U32(0xFFFF0000), _F32)
    return lo, hi


def _store_slabs(ref, first, words):
    m = words.shape[0]
    for s in range(_PITCH):
        piece = words[:, s * _LANES:(s + 1) * _LANES] if s < _SLAB else jnp.zeros((m, _LANES), words.dtype)
        ref[pl.ds(first * _PITCH + s, m, stride=_PITCH), :] = piece


def _load_slabs(ref, first, m):
    return jnp.concatenate([ref[pl.ds(first * _PITCH + s, m, stride=_PITCH), :] for s in range(_SLAB)], axis=1)


def _slab_copy(src_ref, src_slab, dst_ref, dst_slab, sem, rows):
    src = src_ref.at[pl.ds(pl.multiple_of(src_slab * _PITCH, 8), rows)]
    dst = dst_ref.at[pl.ds(pl.multiple_of(dst_slab * _PITCH, 8), rows)]
    return pltpu.make_async_copy(src, dst, sem)


def _wait_slabs(src_ref, dst_ref, n_slabs, sem, rows):
    pltpu.make_async_copy(src_ref.at[pl.ds(0, n_slabs * rows)], dst_ref.at[pl.ds(0, n_slabs * rows)], sem).wait()


def _add_ln_kernel(alpha, res_ref, y_ref, g_ref, b_ref, o32_ref, o16_ref, opk_ref):
    o = _layer_norm(alpha * res_ref[...] + y_ref[...], g_ref[...], b_ref[...])
    o32_ref[...] = o
    o16_ref[...] = o.astype(_BF16)
    _store_slabs(opk_ref, 0, _pack_halves(o))


def _add_ln(res, y, g, b, alpha, tm):
    m, d = res.shape
    tm = min(tm, m)
    tile = pl.BlockSpec((tm, d), lambda i: (i, 0))
    half = pl.BlockSpec((tm * _PITCH, _LANES), lambda i: (i, 0))
    row = pl.BlockSpec((1, d), lambda i: (0, 0))
    return pl.pallas_call(
        functools.partial(_add_ln_kernel, alpha),
        out_shape=(jax.ShapeDtypeStruct((m, d), _F32), jax.ShapeDtypeStruct((m, d), _BF16),
                   jax.ShapeDtypeStruct((m * _PITCH, _LANES), _U32)),
        grid=(m // tm,),
        in_specs=[tile, tile, row, row],
        out_specs=(tile, tile, half),
        compiler_params=_params(1),
        name="residual_layernorm",
    )(res, y, g, b)


def _first_argmax(x, iota, n):
    m = jnp.max(x, axis=0, keepdims=True)
    return jnp.min(jnp.where(x == m, iota, n), axis=0, keepdims=True)


def _router_kernel(h_ref, wr_ref, bias_ref, tri_ref, e_ref, w_ref, r_ref, cnt_ref, carry_ref):
    tm = h_ref.shape[0]

    @pl.when(pl.program_id(0) == 0)
    def _():
        carry_ref[...] = jnp.zeros_like(carry_ref)

    logits = jnp.dot(h_ref[...], wr_ref[...], precision=lax.Precision.HIGHEST,
                     preferred_element_type=_F32)
    scores = jax.nn.sigmoid(logits.T[:_N_EXPERTS])
    bias = jnp.concatenate([bias_ref[...]] * (tm // _LANES), axis=1)
    sel = scores + bias
    ninf = _F32(-jnp.inf)

    iota_g = lax.broadcasted_iota(jnp.int32, (_EGROUP, tm), 0)
    grp = []
    for g in range(_N_EGROUPS):
        x = sel[g * _EGROUP:(g + 1) * _EGROUP]
        m1 = jnp.max(x, axis=0, keepdims=True)
        hit = iota_g == _first_argmax(x, iota_g, _EGROUP)
        m2 = jnp.max(jnp.where(hit, ninf, x), axis=0, keepdims=True)
        grp.append(m1 + m2)
    grp = jnp.concatenate(grp, axis=0)

    iota_n = lax.broadcasted_iota(jnp.int32, (_N_EGROUPS, tm), 0)
    gmask = jnp.zeros((_N_EGROUPS, tm), _F32)
    for _ in range(_TOPK_GROUPS):
        hit = iota_n == _first_argmax(grp, iota_n, _N_EGROUPS)
        gmask = jnp.where(hit, 1.0, gmask)
        grp = jnp.where(hit, ninf, grp)

    masked = jnp.concatenate(
        [jnp.where(gmask[g:g + 1] > 0, sel[g * _EGROUP:(g + 1) * _EGROUP], ninf) for g in range(_N_EGROUPS)], axis=0)
    iota_e = lax.broadcasted_iota(jnp.int32, (_N_EXPERTS, tm), 0)
    chosen = jnp.zeros((_N_EXPERTS, tm), _F32)
    picks = []
    for _ in range(_TOP_K):
        idx = _first_argmax(masked, iota_e, _N_EXPERTS)
        hit = iota_e == idx
        chosen = jnp.where(hit, 1.0, chosen)
        masked = jnp.where(hit, ninf, masked)
        picks.append(idx)

    w = scores * chosen
    gates = w / jnp.sum(w, axis=0, keepdims=True) * _ROUTED_SCALE

    carry = carry_ref[...]
    rank = carry[:, 0:1] + jnp.dot(chosen.astype(_BF16), tri_ref[...], preferred_element_type=_F32)
    carry = carry + jnp.sum(chosen, axis=1, keepdims=True)
    carry_ref[...] = carry
    cnt_ref[...] = carry

    pick_w, pick_r = [], []
    for idx in picks:
        hit = iota_e == idx
        pick_w.append(jnp.sum(jnp.where(hit, gates, 0.0), axis=0, keepdims=True))
        pick_r.append(jnp.sum(jnp.where(hit, rank, 0.0), axis=0, keepdims=True))
    e_ref[...] = jnp.concatenate(picks, axis=0)
    w_ref[...] = jnp.concatenate(pick_w, axis=0)
    r_ref[...] = jnp.concatenate(pick_r, axis=0).astype(jnp.int32)


def _router(h, w_router_p, bias_p, tm):
    m, d = h.shape
    tm = min(tm, m)
    tri = jnp.asarray(np.triu(np.ones((tm, tm), np.float32), 1), _BF16)
    slot = pl.BlockSpec((_TOP_K, tm), lambda i: (0, i))
    return pl.pallas_call(
        _router_kernel,
        out_shape=(jax.ShapeDtypeStruct((_TOP_K, m), jnp.int32), jax.ShapeDtypeStruct((_TOP_K, m), _F32),
                   jax.ShapeDtypeStruct((_TOP_K, m), jnp.int32), jax.ShapeDtypeStruct((_N_EXPERTS, _LANES), _F32)),
        grid=(m // tm,),
        in_specs=[pl.BlockSpec((tm, d), lambda i: (i, 0)),
                  pl.BlockSpec((d, _LANES), lambda i: (0, 0)),
                  pl.BlockSpec((_N_EXPERTS, _LANES), lambda i: (0, 0)),
                  pl.BlockSpec((tm, tm), lambda i: (0, 0))],
        out_specs=(slot, slot, slot, pl.BlockSpec((_N_EXPERTS, _LANES), lambda i: (0, 0))),
        scratch_shapes=[pltpu.VMEM((_N_EXPERTS, _LANES), _F32)],
        compiler_params=_params(1),
        name="moe_router",
    )(h, w_router_p, bias_p, tri)


def _pad_chunks():
    return [1 << b for b in range(_EXPERT_TILE.bit_length() - 1)]


def _dispatch_kernel(pad_start_ref, pad_len_ref, pos_ref, x_ref, xs_ref, zero_ref, sem, zsem):
    tm = x_ref.shape[0] // _PITCH

    def start(t, c):
        for k in range(_TOP_K):
            _slab_copy(x_ref, t, xs_ref, pos_ref[t * _TOP_K + k], sem, _PITCH).start(priority=k % 2)
        return c

    lax.fori_loop(0, tm, start, 0)

    @pl.when(pl.program_id(0) == 0)
    def _():
        zero_ref[...] = jnp.zeros_like(zero_ref)

        def fill(e, do_start):
            cur = pad_start_ref[e]
            n = pad_len_ref[e]
            for c in _pad_chunks():
                @pl.when((n & c) != 0)
                def _(cur=cur, c=c):
                    dst = xs_ref.at[pl.ds(pl.multiple_of(cur * _PITCH, 8), c * _PITCH)]
                    cp = pltpu.make_async_copy(zero_ref.at[pl.ds(0, c * _PITCH)], dst, zsem)
                    if do_start:
                        cp.start()
                    else:
                        cp.wait()
                cur = cur + (n & c)

        n_pads = pad_start_ref.shape[0]
        lax.fori_loop(0, n_pads, lambda e, c: (fill(e, True), c)[1], 0)
        lax.fori_loop(0, n_pads, lambda e, c: (fill(e, False), c)[1], 0)

    for _ in range(_TOP_K):
        _wait_slabs(x_ref, xs_ref, tm, sem, _PITCH)


def _dispatch(xpk, pos_flat, pad_start, pad_len, n_rows):
    tm = _GATHER_TOKENS
    return pl.pallas_call(
        _dispatch_kernel,
        out_shape=jax.ShapeDtypeStruct((n_rows * _PITCH, _LANES), _U32),
        grid_spec=pltpu.PrefetchScalarGridSpec(
            num_scalar_prefetch=2,
            grid=(xpk.shape[0] // (tm * _PITCH),),
            in_specs=[pl.BlockSpec((tm * _TOP_K,), lambda i, ps, pn: (i,), memory_space=pltpu.SMEM),
                      pl.BlockSpec((tm * _PITCH, _LANES), lambda i, ps, pn: (i, 0))],
            out_specs=pl.BlockSpec(memory_space=pl.ANY),
            scratch_shapes=[pltpu.VMEM((_EXPERT_TILE // 2 * _PITCH, _LANES), _U32),
                            pltpu.SemaphoreType.DMA(()), pltpu.SemaphoreType.DMA(())],
        ),
        compiler_params=_params(1),
        name="moe_dispatch",
    )(pad_start, pad_len, pos_flat, xpk)


def _experts_kernel(te_ref, nu_ref, xs_ref, wg_ref, wu_ref, wd_ref, ys_ref):
    @pl.when(pl.program_id(0) >= nu_ref[0])
    def _():
        ys_ref[...] = jnp.zeros_like(ys_ref)

    @pl.when(pl.program_id(0) < nu_ref[0])
    def _():
        k = wg_ref.shape[0] // 2
        rows = _EXPERT_TILE
        gu = jnp.zeros((rows, 2 * _EXPERT_FF), _F32)
        for s in range(_SLAB):
            c = slice(s * _LANES, (s + 1) * _LANES)
            ch = slice(k + s * _LANES, k + (s + 1) * _LANES)
            lo, hi = _unpack_halves(xs_ref[pl.ds(s, rows, stride=_PITCH), :])
            x_s = jnp.concatenate([lo, hi], axis=1).astype(_BF16)
            w_s = jnp.concatenate([jnp.concatenate([wg_ref[c], wu_ref[c]], axis=1),
                                   jnp.concatenate([wg_ref[ch], wu_ref[ch]], axis=1)], axis=0)
            gu += jnp.dot(x_s, w_s, preferred_element_type=_F32)
        act = (jax.nn.silu(gu[:, :_EXPERT_FF]) * gu[:, _EXPERT_FF:]).astype(_BF16)
        for s in range(_SLAB):
            c = slice(s * _LANES, (s + 1) * _LANES)
            ch = slice(k + s * _LANES, k + (s + 1) * _LANES)
            y_s = jnp.dot(act, jnp.concatenate([wd_ref[:, c], wd_ref[:, ch]], axis=1), preferred_element_type=_F32)
            ys_ref[pl.ds(s, rows, stride=_PITCH), :] = _pack_halves(y_s)
        for s in range(_SLAB, _PITCH):
            ys_ref[pl.ds(s, rows, stride=_PITCH), :] = jnp.zeros((rows, _LANES), _U32)


def _experts(xs, wg, wu, wd, layer, tile_expert, n_used):
    n_rows = xs.shape[0] // _PITCH
    d = wd.shape[3]
    w_in = pl.BlockSpec((None, None, d, _EXPERT_FF), lambda t, te, nu: (layer, te[t], 0, 0))
    tile = lambda t, te, nu: (t, 0)
    return pl.pallas_call(
        _experts_kernel,
        out_shape=jax.ShapeDtypeStruct(xs.shape, _U32),
        grid_spec=pltpu.PrefetchScalarGridSpec(
            num_scalar_prefetch=2,
            grid=(n_rows // _EXPERT_TILE,),
            in_specs=[pl.BlockSpec((_EXPERT_TILE * _PITCH, _LANES), tile),
                      w_in, w_in,
                      pl.BlockSpec((None, None, _EXPERT_FF, d), lambda t, te, nu: (layer, te[t], 0, 0))],
            out_specs=pl.BlockSpec((_EXPERT_TILE * _PITCH, _LANES), tile),
        ),
        compiler_params=_params(1),
        name="moe_experts",
    )(tile_expert, n_used, xs, wg, wu, wd)


def _mlp_kernel(x_ref, wg_ref, wu_ref, wd_ref, o_ref):
    j = pl.program_id(1)
    x = x_ref[...]
    hg = jnp.dot(x, wg_ref[...], preferred_element_type=_F32)
    hu = jnp.dot(x, wu_ref[...], preferred_element_type=_F32)
    contrib = jnp.dot((jax.nn.silu(hg) * hu).astype(_BF16), wd_ref[...], preferred_element_type=_F32)

    @pl.when(j == 0)
    def _():
        o_ref[...] = contrib

    @pl.when(j > 0)
    def _():
        o_ref[...] += contrib


def _mlp(xb, wg, wu, wd, layer, tm, tf):
    m, d = xb.shape
    ff = wg.shape[2]
    tm, tf = min(tm, m), min(tf, ff)
    return pl.pallas_call(
        _mlp_kernel,
        out_shape=jax.ShapeDtypeStruct((m, d), _F32),
        grid=(m // tm, ff // tf),
        in_specs=[pl.BlockSpec((tm, d), lambda i, j: (i, 0)),
                  pl.BlockSpec((None, d, tf), lambda i, j: (layer, 0, j)),
                  pl.BlockSpec((None, d, tf), lambda i, j: (layer, 0, j)),
                  pl.BlockSpec((None, tf, d), lambda i, j: (layer, j, 0))],
        out_specs=pl.BlockSpec((tm, d), lambda i, j: (i, 0)),
        compiler_params=_params(2),
        name="shared_expert",
    )(xb, wg, wu, wd)


def _combine_kernel(alpha, pos_ref, pos_next_ref, w_ref, ys_ref, sh_ref, h_ref, g_ref, b_ref,
                    o32_ref, o16_ref, buf_ref, sem):
    tm = h_ref.shape[0]
    i = pl.program_id(0)
    slot = i % 2

    def gather(p_ref, into):
        def start(t, c):
            for k in range(_TOP_K):
                dst = (into * _TOP_K + k) * tm + t
                _slab_copy(ys_ref, p_ref[t * _TOP_K + k], buf_ref, dst, sem.at[into], _SLAB).start(priority=k % 2)
            return c
        lax.fori_loop(0, tm, start, 0)

    @pl.when(i == 0)
    def _():
        gather(pos_ref, 0)

    @pl.when(i + 1 < pl.num_programs(0))
    def _():
        gather(pos_next_ref, 1 - slot)

    _wait_slabs(ys_ref, buf_ref, _TOP_K * tm, sem.at[slot], _SLAB)

    acc_lo = acc_hi = None
    for k in range(_TOP_K):
        lo, hi = _unpack_halves(_load_slabs(buf_ref, (slot * _TOP_K + k) * tm, tm))
        wk = w_ref[:, k:k + 1]
        acc_lo = lo * wk if acc_lo is None else acc_lo + lo * wk
        acc_hi = hi * wk if acc_hi is None else acc_hi + hi * wk
    moe = jnp.concatenate([acc_lo, acc_hi], axis=1) + sh_ref[...]
    o = _layer_norm(alpha * h_ref[...] + moe, g_ref[...], b_ref[...])
    o32_ref[...] = o
    o16_ref[...] = o.astype(_BF16)


def _combine(ys, pos_flat, w_tok, shared, h, g, b, alpha):
    m, d = h.shape
    tm = _GATHER_TOKENS
    tile = pl.BlockSpec((tm, d), lambda i: (i, 0))
    row = pl.BlockSpec((1, d), lambda i: (0, 0))
    return pl.pallas_call(
        functools.partial(_combine_kernel, alpha),
        out_shape=(jax.ShapeDtypeStruct((m, d), _F32), jax.ShapeDtypeStruct((m, d), _BF16)),
        grid=(m // tm,),
        in_specs=[pl.BlockSpec((tm * _TOP_K,), lambda i: (i,), memory_space=pltpu.SMEM),
                  pl.BlockSpec((tm * _TOP_K,), lambda i: (jnp.minimum(i + 1, m // tm - 1),), memory_space=pltpu.SMEM),
                  pl.BlockSpec((tm, _TOP_K), lambda i: (i, 0)),
                  pl.BlockSpec(memory_space=pl.ANY),
                  tile, tile, row, row],
        out_specs=(tile, tile),
        scratch_shapes=[pltpu.VMEM((2 * _TOP_K * tm * _PITCH, _LANES), _U32), pltpu.SemaphoreType.DMA((2,))],
        compiler_params=_params(1),
        name="moe_combine_layernorm",
    )(pos_flat, pos_flat, w_tok, ys, shared, h, g, b)


def _routing_plan(e_t, r_t, counts, n_tiles):
    counts = counts.astype(jnp.int32)
    padded = (counts + _EXPERT_TILE - 1) // _EXPERT_TILE * _EXPERT_TILE
    ends = jnp.cumsum(padded)
    offsets = ends - padded
    experts = jnp.arange(_N_EXPERTS, dtype=jnp.int32)
    pos = r_t + jnp.sum(jnp.where(e_t[..., None] == experts, offsets, 0), axis=-1)
    tile_start = jnp.arange(n_tiles, dtype=jnp.int32) * _EXPERT_TILE
    tile_expert = jnp.sum((ends[None, :] <= tile_start[:, None]).astype(jnp.int32), axis=1)
    tile_expert = jnp.minimum(tile_expert, _N_EXPERTS - 1)
    n_used = (ends[-1:] // _EXPERT_TILE).astype(jnp.int32)
    half = _EXPERT_TILE // 2
    tail_start = ends[-1] + jnp.arange(2 * _N_EXPERTS, dtype=jnp.int32) * half
    tail_len = jnp.where(tail_start < n_tiles * _EXPERT_TILE, half, 0)
    pad_start = jnp.concatenate([offsets + counts, tail_start]).astype(jnp.int32)
    pad_len = jnp.concatenate([padded - counts, tail_len]).astype(jnp.int32)
    return pos.T.reshape(-1), tile_expert, n_used, pad_start, pad_len


def _ple_kernel(hb_ref, h_ref, p_ref, wpg_ref, wple_ref, o32_ref, o16_ref):
    gate = jax.nn.sigmoid(jnp.dot(hb_ref[...], wpg_ref[...], preferred_element_type=_F32))
    pe = jnp.dot(p_ref[...].astype(_BF16), wple_ref[...], preferred_element_type=_F32)
    o = h_ref[...] + gate * pe
    o32_ref[...] = o
    o16_ref[...] = o.astype(_BF16)


def _ple(hb, h, p, wpg, wple, layer, row0, rows, tm, tn):
    d = hb.shape[1]
    pd = p.shape[1]
    tm, tn = min(tm, rows), min(tn, d)
    i0 = row0 // tm
    assert row0 % tm == 0 and rows % tm == 0
    out_tile = pl.BlockSpec((tm, tn), lambda i, j: (i, j))
    return pl.pallas_call(
        _ple_kernel,
        out_shape=(jax.ShapeDtypeStruct((rows, d), _F32), jax.ShapeDtypeStruct((rows, d), _BF16)),
        grid=(rows // tm, d // tn),
        in_specs=[pl.BlockSpec((tm, d), lambda i, j: (i0 + i, 0)),
                  pl.BlockSpec((tm, tn), lambda i, j: (i0 + i, j)),
                  pl.BlockSpec((tm, pd), lambda i, j: (i0 + i, 0)),
                  pl.BlockSpec((None, d, tn), lambda i, j: (layer, 0, j)),
                  pl.BlockSpec((None, pd, tn), lambda i, j: (layer, 0, j))],
        out_specs=(out_tile, out_tile),
        compiler_params=_params(2),
        name="ple_gate",
    )(hb, h, p, wpg, wple)


def _layer(x, xb, p, image_rows, alpha, layer, out_rows, w, wb):
    (attn_rpb, gmlp_ln_g, gmlp_ln_b, gmlp_bs, mix_norm_g, ln1_g, ln1_b, w_router, router_bias, ln2_g, ln2_b) = w
    (w_in, gmlp_ws, w_out, w_e_gate, w_e_up, w_e_down, w_sh_gate, w_sh_up, w_sh_down, w_ple, w_ple_gate) = wb
    row = lambda v: v.reshape(1, -1).astype(_F32)

    proj = _matmul(xb, w_in, layer, _BF16, 1024, 1024, "in_proj")

    rs_tbl, d0_tbl = _attn_tables(image_rows)
    a = _attention(proj, _attn_bias_table(attn_rpb), row(mix_norm_g[:_ATTN_W]),
                   jnp.asarray(rs_tbl), jnp.asarray(d0_tbl))

    bsx = jnp.repeat(gmlp_bs.T.astype(_F32), _HEAD_DIM, axis=1)
    g = _gmlp(proj, gmlp_ws[layer], row(gmlp_ln_g), row(gmlp_ln_b), bsx, row(mix_norm_g[_ATTN_W:]))

    mixed_out = _out_proj(a, g, w_out, layer, 1024, 1024)
    h, hb, hpk = _add_ln(x, mixed_out, row(ln1_g), row(ln1_b), alpha, 256)

    wr_p = jnp.pad(w_router.astype(_F32), ((0, 0), (0, _LANES - _N_EXPERTS)))
    bias_p = jnp.broadcast_to(router_bias.astype(_F32)[:, None], (_N_EXPERTS, _LANES))
    e_t, w_t, r_t, counts = _router(h, wr_p, bias_p, 512)

    n_rows = x.shape[0] * _TOP_K + _N_EXPERTS * _EXPERT_TILE
    pos_flat, tile_expert, n_used, pad_start, pad_len = _routing_plan(
        e_t, r_t, counts[:, 0], n_rows // _EXPERT_TILE)
    xs = _dispatch(hpk, pos_flat, pad_start, pad_len, n_rows)
    ys = _experts(xs, w_e_gate, w_e_up, w_e_down, layer, tile_expert, n_used)
    shared = _mlp(hb, w_sh_gate, w_sh_up, w_sh_down, layer, 512, 512)
    h2, h2b = _combine(ys, pos_flat, w_t.T, shared, h, row(ln2_g), row(ln2_b), alpha)
    return [_ple(h2b, h2, p, w_ple_gate, w_ple, layer, r0, nr, 1024, 512) for r0, nr in out_rows]


def kernel(x_prompt, x_sample, p_prompt, p_sample, w_in, attn_rpb, gmlp_ln_g, gmlp_ln_b, gmlp_ws, gmlp_bs,
           mix_norm_g, w_out, ln1_g, ln1_b, w_router, router_bias, w_e_gate, w_e_up, w_e_down,
           w_sh_gate, w_sh_up, w_sh_down, ln2_g, ln2_b, w_ple, w_ple_gate):
    depth = w_in.shape[0]
    d = x_prompt.shape[-1]
    alpha = (2 * depth) ** 0.25
    groups = (x_prompt, x_sample)
    image_rows = [xg.shape[1] // _GRID_W for xg in groups for _ in range(xg.shape[0])]
    x = jnp.concatenate([xg.reshape(-1, d) for xg in groups], axis=0)
    xb = x.astype(_BF16)
    n, n_prompt = x.shape[0], x_prompt.shape[0] * x_prompt.shape[1]
    small = (attn_rpb, gmlp_ln_g, gmlp_ln_b, gmlp_bs, mix_norm_g, ln1_g, ln1_b, w_router, router_bias, ln2_g, ln2_b)
    wb = tuple(wt.astype(_BF16) for wt in (w_in, gmlp_ws, w_out, w_e_gate, w_e_up, w_e_down,
                                           w_sh_gate, w_sh_up, w_sh_down, w_ple, w_ple_gate))
    for i in range(depth):
        p = jnp.concatenate([pg[i].reshape(-1, pg.shape[-1]) for pg in (p_prompt, p_sample)], axis=0)
        last = i == depth - 1
        out_rows = [(0, n_prompt), (n_prompt, n - n_prompt)] if last else [(0, n)]
        outs = _layer(x, xb, p, image_rows, alpha, i, out_rows, tuple(wt[i] for wt in small), wb)
        x, xb = outs[0]
    return outs[0][0].reshape(x_prompt.shape), outs[1][0].reshape(x_sample.shape)
```

```python
import functools

import numpy as np
import jax
import jax.numpy as jnp
from jax import lax
from jax.experimental import pallas as pl
from jax.experimental.pallas import tpu as pltpu

_F32 = jnp.float32
_BF16 = jnp.bfloat16
_U32 = jnp.uint32

_HEAD_DIM = 128
_N_HEADS = 16
_N_GROUPS = 16
_ATTN_W = _N_HEADS * _HEAD_DIM
_GMLP_W = _N_GROUPS * _HEAD_DIM
_CHUNK = 128
_GRID_W = 64
_WIN_H = 8
_WIN_W = 16
_N_EXPERTS = 64
_TOP_K = 8
_N_EGROUPS = 8
_EGROUP = _N_EXPERTS // _N_EGROUPS
_TOPK_GROUPS = 4
_EXPERT_FF = 128
_ROUTED_SCALE = 2.5
_LN_EPS = 1e-5
_NEG_INF = -1e30
_LANES = 128
_EXPERT_TILE = 256
_GATHER_TOKENS = 128
_SLAB = 16
_PITCH = 24

_V7X_VMEM_LIMIT = 56 * 1024 * 1024


def _params(n_axes):
    return pltpu.CompilerParams(dimension_semantics=("arbitrary",) * n_axes,
                                vmem_limit_bytes=_V7X_VMEM_LIMIT)


def _mm_kernel(x_ref, w_ref, o_ref):
    o_ref[...] = jnp.dot(x_ref[...], w_ref[...], preferred_element_type=_F32).astype(o_ref.dtype)


def _matmul(x, w, layer, out_dtype, tm, tn, name):
    m, k = x.shape
    n = w.shape[2]
    tm, tn = min(tm, m), min(tn, n)
    return pl.pallas_call(
        _mm_kernel,
        out_shape=jax.ShapeDtypeStruct((m, n), out_dtype),
        grid=(m // tm, n // tn),
        in_specs=[pl.BlockSpec((tm, k), lambda i, j: (i, 0)),
                  pl.BlockSpec((None, k, tn), lambda i, j: (layer, 0, j))],
        out_specs=pl.BlockSpec((tm, tn), lambda i, j: (i, j)),
        compiler_params=_params(2),
        name=name,
    )(x, w)


def _mm2_kernel(a_ref, g_ref, wa_ref, wg_ref, o_ref):
    acc = jnp.dot(a_ref[...], wa_ref[...], preferred_element_type=_F32)
    acc += jnp.dot(g_ref[...], wg_ref[...], preferred_element_type=_F32)
    o_ref[...] = acc


def _out_proj(a, g, w_out, layer, tm, tn):
    m, ka = a.shape
    kg = g.shape[1]
    n = w_out.shape[2]
    tm, tn = min(tm, m), min(tn, n)
    return pl.pallas_call(
        _mm2_kernel,
        out_shape=jax.ShapeDtypeStruct((m, n), _F32),
        grid=(m // tm, n // tn),
        in_specs=[pl.BlockSpec((tm, ka), lambda i, j: (i, 0)),
                  pl.BlockSpec((tm, kg), lambda i, j: (i, 0)),
                  pl.BlockSpec((None, ka, tn), lambda i, j: (layer, 0, j)),
                  pl.BlockSpec((None, kg, tn), lambda i, j: (layer, 1, j))],
        out_specs=pl.BlockSpec((tm, tn), lambda i, j: (i, j)),
        compiler_params=_params(2),
        name="out_proj",
    )(a, g, w_out, w_out)


def _attn_kernel(rs_ref, d0_ref, q_ref, *rest):
    del rs_ref, d0_ref
    k_refs = rest[:_WIN_H]
    v_refs = rest[_WIN_H:2 * _WIN_H]
    bias_ref, gain_ref, o_ref, acc_ref, p_ref = rest[2 * _WIN_H:]
    scale = _HEAD_DIM ** -0.5
    norm = []
    for h in range(_N_HEADS):
        sl = slice(h * _HEAD_DIM, (h + 1) * _HEAD_DIM)
        q = q_ref[:, sl]
        k = jnp.concatenate([r[:, sl] for r in k_refs], axis=0)
        s = lax.dot_general(q, k, (((1,), (1,)), ((), ())), preferred_element_type=_F32)
        s = s * scale + bias_ref[h]
        m = jnp.max(s, axis=-1, keepdims=True)
        p = jnp.exp(s - m)
        norm.append(jnp.sum(p, axis=-1, keepdims=True))
        p_ref[h] = p.astype(_BF16)
    ss = jnp.zeros((_GRID_W, 1), _F32)
    for h in range(_N_HEADS):
        sl = slice(h * _HEAD_DIM, (h + 1) * _HEAD_DIM)
        v = jnp.concatenate([r[:, sl] for r in v_refs], axis=0)
        o = jnp.dot(p_ref[h], v, preferred_element_type=_F32) / norm[h]
        acc_ref[:, sl] = o
        ss = ss + jnp.sum(o * o, axis=-1, keepdims=True)
    r = lax.rsqrt(ss / _ATTN_W + _LN_EPS)
    o_ref[...] = (acc_ref[...] * r * gain_ref[...]).astype(o_ref.dtype)


def _attention(proj, bias_tbl, gain_a, rs_tbl, d0_tbl):
    n = proj.shape[0]
    nr = n // _GRID_W
    proj3 = proj.reshape(nr, _GRID_W, proj.shape[1])
    blk = (None, _GRID_W, _ATTN_W)
    in_specs = [pl.BlockSpec(blk, lambda r, rs, d0: (r, 0, 0))]
    for c in (1, 2):
        for i in range(_WIN_H):
            in_specs.append(pl.BlockSpec(blk, lambda r, rs, d0, i=i, c=c: (rs[r] + i, 0, c)))
    in_specs.append(pl.BlockSpec((None, _N_HEADS, _GRID_W, _WIN_H * _GRID_W),
                                 lambda r, rs, d0: (d0[r], 0, 0, 0)))
    in_specs.append(pl.BlockSpec((1, _ATTN_W), lambda r, rs, d0: (0, 0)))
    out = pl.pallas_call(
        _attn_kernel,
        out_shape=jax.ShapeDtypeStruct((nr, _GRID_W, _ATTN_W), _BF16),
        grid_spec=pltpu.PrefetchScalarGridSpec(
            num_scalar_prefetch=2,
            grid=(nr,),
            in_specs=in_specs,
            out_specs=pl.BlockSpec(blk, lambda r, rs, d0: (r, 0, 0)),
            scratch_shapes=[pltpu.VMEM((_GRID_W, _ATTN_W), _F32),
                            pltpu.VMEM((_N_HEADS, _GRID_W, _WIN_H * _GRID_W), _BF16)],
        ),
        compiler_params=_params(1),
        name="na2d_attention",
    )(rs_tbl, d0_tbl, *([proj3] * (1 + 2 * _WIN_H)), bias_tbl, gain_a)
    return out.reshape(n, _ATTN_W)


def _attn_tables(image_rows):
    rs_tbl, d0_tbl, base = [], [], 0
    for rows in image_rows:
        kh = min(_WIN_H, rows)
        assert kh == _WIN_H
        for r in range(rows):
            rs = min(max(r - kh // 2, 0), rows - kh)
            rs_tbl.append(base + rs)
            d0_tbl.append(rs - r + _WIN_H - 1)
        base += rows
    return np.asarray(rs_tbl, np.int32), np.asarray(d0_tbl, np.int32)


def _attn_bias_table(rpb):
    cols = np.arange(_GRID_W)
    col_start = np.clip(cols - _WIN_W // 2, 0, _GRID_W - _WIN_W)
    col_mask = (cols[None, :] >= col_start[:, None]) & (cols[None, :] < col_start[:, None] + _WIN_W)
    col_off = np.clip(cols[None, :] - cols[:, None] + (_WIN_W - 1), 0, 2 * _WIN_W - 2)
    rpb_c = jnp.where(col_mask[None, None], rpb[:, :, col_off].astype(_F32), _F32(_NEG_INF))
    tabs = []
    for d0 in range(_WIN_H):
        t = rpb_c[:, d0:d0 + _WIN_H].transpose(0, 2, 1, 3)
        tabs.append(t.reshape(_N_HEADS, _GRID_W, _WIN_H * _GRID_W))
    return jnp.stack(tabs)


def _gmlp_kernel(u_ref, v_ref, ws_ref, lng_ref, lnb_ref, bsx_ref, gain_ref, o_ref, acc_ref):
    ss = jnp.zeros((_CHUNK, 1), _F32)
    for g in range(_N_GROUPS):
        sl = slice(g * _HEAD_DIM, (g + 1) * _HEAD_DIM)
        v = jax.nn.gelu(v_ref[:, sl].astype(_F32))
        mu = jnp.mean(v, axis=-1, keepdims=True)
        c = v - mu
        var = jnp.mean(c * c, axis=-1, keepdims=True)
        vn = c * lax.rsqrt(var + _LN_EPS) * lng_ref[:, sl] + lnb_ref[:, sl]
        mixed = jnp.dot(ws_ref[g], vn.astype(_BF16), preferred_element_type=_F32) + bsx_ref[:, sl]
        o = jax.nn.gelu(u_ref[:, sl].astype(_F32)) * mixed
        acc_ref[:, sl] = o
        ss = ss + jnp.sum(o * o, axis=-1, keepdims=True)
    r = lax.rsqrt(ss / _GMLP_W + _LN_EPS)
    o_ref[...] = (acc_ref[...] * r * gain_ref[...]).astype(o_ref.dtype)


def _gmlp(proj, ws, ln_g, ln_b, bsx, gain_g):
    n = proj.shape[0]
    row = pl.BlockSpec((1, _GMLP_W), lambda i: (0, 0))
    return pl.pallas_call(
        _gmlp_kernel,
        out_shape=jax.ShapeDtypeStruct((n, _GMLP_W), _BF16),
        grid=(n // _CHUNK,),
        in_specs=[pl.BlockSpec((_CHUNK, _GMLP_W), lambda i: (i, 3)),
                  pl.BlockSpec((_CHUNK, _GMLP_W), lambda i: (i, 4)),
                  pl.BlockSpec((_N_GROUPS, _CHUNK, _CHUNK), lambda i: (0, 0, 0)),
                  row, row,
                  pl.BlockSpec((_CHUNK, _GMLP_W), lambda i: (0, 0)),
                  row],
        out_specs=pl.BlockSpec((_CHUNK, _GMLP_W), lambda i: (i, 0)),
        scratch_shapes=[pltpu.VMEM((_CHUNK, _GMLP_W), _F32)],
        compiler_params=_params(1),
        name="gmlp_gating",
    )(proj, proj, ws, ln_g, ln_b, bsx, gain_g)


def _layer_norm(z, g, b):
    mu = jnp.mean(z, axis=-1, keepdims=True)
    c = z - mu
    var = jnp.mean(c * c, axis=-1, keepdims=True)
    return c * lax.rsqrt(var + _LN_EPS) * g + b


def _pack_halves(x):
    k = x.shape[1] // 2
    bits = lambda v: lax.bitcast_convert_type(v.astype(_BF16).astype(_F32), _U32)
    return (bits(x[:, k:]) & _U32(0xFFFF0000)) | (bits(x[:, :k]) >> _U32(16))


def _unpack_halves(words):
    lo = lax.bitcast_convert_type(words << _U32(16), _F32)
    hi = lax.bitcast_convert_type(words & _U32(0xFFFF0000), _F32)
    return lo, hi


def _store_slabs(ref, words):
    m = words.shape[0]
    for s in range(_PITCH):
        piece = words[:, s * _LANES:(s + 1) * _LANES] if s < _SLAB else jnp.zeros((m, _LANES), words.dtype)
        ref[pl.ds(s, m, stride=_PITCH), :] = piece


def _load_slabs(ref, first, m):
    return jnp.concatenate([ref[pl.ds(first * _PITCH + s, m, stride=_PITCH), :] for s in range(_SLAB)], axis=1)


def _slab_copy(src_ref, src_slab, dst_ref, dst_slab, sem, rows):
    src = src_ref.at[pl.ds(pl.multiple_of(src_slab * _PITCH, 8), rows)]
    dst = dst_ref.at[pl.ds(pl.multiple_of(dst_slab * _PITCH, 8), rows)]
    return pltpu.make_async_copy(src, dst, sem)


def _wait_slabs(src_ref, dst_ref, n_slabs, sem, rows):
    pltpu.make_async_copy(src_ref.at[pl.ds(0, n_slabs * rows)], dst_ref.at[pl.ds(0, n_slabs * rows)], sem).wait()


def _add_ln_kernel(alpha, res_ref, y_ref, g_ref, b_ref, o32_ref, o16_ref, opk_ref):
    o = _layer_norm(alpha * res_ref[...] + y_ref[...], g_ref[...], b_ref[...])
    o32_ref[...] = o
    o16_ref[...] = o.astype(_BF16)
    _store_slabs(opk_ref, _pack_halves(o))


def _add_ln(res, y, g, b, alpha, tm):
    m, d = res.shape
    tm = min(tm, m)
    tile = pl.BlockSpec((tm, d), lambda i: (i, 0))
    half = pl.BlockSpec((tm * _PITCH, _LANES), lambda i: (i, 0))
    row = pl.BlockSpec((1, d), lambda i: (0, 0))
    return pl.pallas_call(
        functools.partial(_add_ln_kernel, alpha),
        out_shape=(jax.ShapeDtypeStruct((m, d), _F32), jax.ShapeDtypeStruct((m, d), _BF16),
                   jax.ShapeDtypeStruct((m * _PITCH, _LANES), _U32)),
        grid=(m // tm,),
        in_specs=[tile, tile, row, row],
        out_specs=(tile, tile, half),
        compiler_params=_params(1),
        name="residual_layernorm",
    )(res, y, g, b)


def _first_argmax(x, iota, n):
    m = jnp.max(x, axis=0, keepdims=True)
    return jnp.min(jnp.where(x == m, iota, n), axis=0, keepdims=True)


def _router_kernel(h_ref, wr_ref, bias_ref, tri_ref, e_ref, w_ref, r_ref, cnt_ref, carry_ref):
    tm = h_ref.shape[0]

    @pl.when(pl.program_id(0) == 0)
    def _():
        carry_ref[...] = jnp.zeros_like(carry_ref)

    logits = jnp.dot(h_ref[...], wr_ref[...], precision=lax.Precision.HIGHEST,
                     preferred_element_type=_F32)
    scores = jax.nn.sigmoid(logits.T[:_N_EXPERTS])
    bias = jnp.concatenate([bias_ref[...]] * (tm // _LANES), axis=1)
    sel = scores + bias
    ninf = _F32(-jnp.inf)

    iota_g = lax.broadcasted_iota(jnp.int32, (_EGROUP, tm), 0)
    grp = []
    for g in range(_N_EGROUPS):
        x = sel[g * _EGROUP:(g + 1) * _EGROUP]
        m1 = jnp.max(x, axis=0, keepdims=True)
        hit = iota_g == _first_argmax(x, iota_g, _EGROUP)
        m2 = jnp.max(jnp.where(hit, ninf, x), axis=0, keepdims=True)
        grp.append(m1 + m2)
    grp = jnp.concatenate(grp, axis=0)

    iota_n = lax.broadcasted_iota(jnp.int32, (_N_EGROUPS, tm), 0)
    gmask = jnp.zeros((_N_EGROUPS, tm), _F32)
    for _ in range(_TOPK_GROUPS):
        hit = iota_n == _first_argmax(grp, iota_n, _N_EGROUPS)
        gmask = jnp.where(hit, 1.0, gmask)
        grp = jnp.where(hit, ninf, grp)

    masked = jnp.concatenate(
        [jnp.where(gmask[g:g + 1] > 0, sel[g * _EGROUP:(g + 1) * _EGROUP], ninf) for g in range(_N_EGROUPS)], axis=0)
    iota_e = lax.broadcasted_iota(jnp.int32, (_N_EXPERTS, tm), 0)
    chosen = jnp.zeros((_N_EXPERTS, tm), _F32)
    picks = []
    for _ in range(_TOP_K):
        idx = _first_argmax(masked, iota_e, _N_EXPERTS)
        hit = iota_e == idx
        chosen = jnp.where(hit, 1.0, chosen)
        masked = jnp.where(hit, ninf, masked)
        picks.append(idx)

    w = scores * chosen
    gates = w / jnp.sum(w, axis=0, keepdims=True) * _ROUTED_SCALE

    carry = carry_ref[...]
    rank = carry[:, 0:1] + jnp.dot(chosen.astype(_BF16), tri_ref[...], preferred_element_type=_F32)
    carry = carry + jnp.sum(chosen, axis=1, keepdims=True)
    carry_ref[...] = carry
    cnt_ref[...] = carry

    pick_w, pick_r = [], []
    for idx in picks:
        hit = iota_e == idx
        pick_w.append(jnp.sum(jnp.where(hit, gates, 0.0), axis=0, keepdims=True))
        pick_r.append(jnp.sum(jnp.where(hit, rank, 0.0), axis=0, keepdims=True))
    e_ref[...] = jnp.concatenate(picks, axis=0)
    w_ref[...] = jnp.concatenate(pick_w, axis=0)
    r_ref[...] = jnp.concatenate(pick_r, axis=0).astype(jnp.int32)


def _router(h, w_router_p, bias_p, tm):
    m, d = h.shape
    tm = min(tm, m)
    tri = jnp.asarray(np.triu(np.ones((tm, tm), np.float32), 1), _BF16)
    slot = pl.BlockSpec((_TOP_K, tm), lambda i: (0, i))
    return pl.pallas_call(
        _router_kernel,
        out_shape=(jax.ShapeDtypeStruct((_TOP_K, m), jnp.int32), jax.ShapeDtypeStruct((_TOP_K, m), _F32),
                   jax.ShapeDtypeStruct((_TOP_K, m), jnp.int32), jax.ShapeDtypeStruct((_N_EXPERTS, _LANES), _F32)),
        grid=(m // tm,),
        in_specs=[pl.BlockSpec((tm, d), lambda i: (i, 0)),
                  pl.BlockSpec((d, _LANES), lambda i: (0, 0)),
                  pl.BlockSpec((_N_EXPERTS, _LANES), lambda i: (0, 0)),
                  pl.BlockSpec((tm, tm), lambda i: (0, 0))],
        out_specs=(slot, slot, slot, pl.BlockSpec((_N_EXPERTS, _LANES), lambda i: (0, 0))),
        scratch_shapes=[pltpu.VMEM((_N_EXPERTS, _LANES), _F32)],
        compiler_params=_params(1),
        name="moe_router",
    )(h, w_router_p, bias_p, tri)


def _pad_chunks():
    return [1 << b for b in range(_EXPERT_TILE.bit_length() - 1)]


def _dispatch_kernel(pad_start_ref, pad_len_ref, pos_ref, x_ref, xs_ref, zero_ref, sem, zsem):
    tm = x_ref.shape[0] // _PITCH

    def start(t, c):
        for k in range(_TOP_K):
            _slab_copy(x_ref, t, xs_ref, pos_ref[t * _TOP_K + k], sem, _PITCH).start(priority=k % 2)
        return c

    lax.fori_loop(0, tm, start, 0)

    @pl.when(pl.program_id(0) == 0)
    def _():
        zero_ref[...] = jnp.zeros_like(zero_ref)

        def fill(e, do_start):
            cur = pad_start_ref[e]
            n = pad_len_ref[e]
            for c in _pad_chunks():
                @pl.when((n & c) != 0)
                def _(cur=cur, c=c):
                    dst = xs_ref.at[pl.ds(pl.multiple_of(cur * _PITCH, 8), c * _PITCH)]
                    cp = pltpu.make_async_copy(zero_ref.at[pl.ds(0, c * _PITCH)], dst, zsem)
                    if do_start:
                        cp.start()
                    else:
                        cp.wait()
                cur = cur + (n & c)

        n_pads = pad_start_ref.shape[0]
        lax.fori_loop(0, n_pads, lambda e, c: (fill(e, True), c)[1], 0)
        lax.fori_loop(0, n_pads, lambda e, c: (fill(e, False), c)[1], 0)

    for _ in range(_TOP_K):
        _wait_slabs(x_ref, xs_ref, tm, sem, _PITCH)


def _dispatch(xpk, pos_flat, pad_start, pad_len, n_rows):
    tm = _GATHER_TOKENS
    return pl.pallas_call(
        _dispatch_kernel,
        out_shape=jax.ShapeDtypeStruct((n_rows * _PITCH, _LANES), _U32),
        grid_spec=pltpu.PrefetchScalarGridSpec(
            num_scalar_prefetch=2,
            grid=(xpk.shape[0] // (tm * _PITCH),),
            in_specs=[pl.BlockSpec((tm * _TOP_K,), lambda i, ps, pn: (i,), memory_space=pltpu.SMEM),
                      pl.BlockSpec((tm * _PITCH, _LANES), lambda i, ps, pn: (i, 0))],
            out_specs=pl.BlockSpec(memory_space=pl.ANY),
            scratch_shapes=[pltpu.VMEM((_EXPERT_TILE // 2 * _PITCH, _LANES), _U32),
                            pltpu.SemaphoreType.DMA(()), pltpu.SemaphoreType.DMA(())],
        ),
        compiler_params=_params(1),
        name="moe_dispatch",
    )(pad_start, pad_len, pos_flat, xpk)


def _experts_kernel(te_ref, nu_ref, xs_ref, wg_ref, wu_ref, wd_ref, ys_ref):
    @pl.when(pl.program_id(0) >= nu_ref[0])
    def _():
        ys_ref[...] = jnp.zeros_like(ys_ref)

    @pl.when(pl.program_id(0) < nu_ref[0])
    def _():
        lo, hi = _unpack_halves(_load_slabs(xs_ref, 0, _EXPERT_TILE))
        k = lo.shape[1]
        w_lo = jnp.concatenate([wg_ref[:k], wu_ref[:k]], axis=1)
        w_hi = jnp.concatenate([wg_ref[k:], wu_ref[k:]], axis=1)
        gu = jnp.dot(lo.astype(_BF16), w_lo, preferred_element_type=_F32)
        gu += jnp.dot(hi.astype(_BF16), w_hi, preferred_element_type=_F32)
        act = jax.nn.silu(gu[:, :_EXPERT_FF]) * gu[:, _EXPERT_FF:]
        y = jnp.dot(act.astype(_BF16), wd_ref[...], preferred_element_type=_F32)
        _store_slabs(ys_ref, _pack_halves(y))


def _experts(xs, wg, wu, wd, layer, tile_expert, n_used):
    n_rows = xs.shape[0] // _PITCH
    d = wd.shape[3]
    w_in = pl.BlockSpec((None, None, d, _EXPERT_FF), lambda t, te, nu: (layer, te[t], 0, 0))
    tile = lambda t, te, nu: (t, 0)
    return pl.pallas_call(
        _experts_kernel,
        out_shape=jax.ShapeDtypeStruct(xs.shape, _U32),
        grid_spec=pltpu.PrefetchScalarGridSpec(
            num_scalar_prefetch=2,
            grid=(n_rows // _EXPERT_TILE,),
            in_specs=[pl.BlockSpec((_EXPERT_TILE * _PITCH, _LANES), tile),
                      w_in, w_in,
                      pl.BlockSpec((None, None, _EXPERT_FF, d), lambda t, te, nu: (layer, te[t], 0, 0))],
            out_specs=pl.BlockSpec((_EXPERT_TILE * _PITCH, _LANES), tile),
        ),
        compiler_params=_params(1),
        name="moe_experts",
    )(tile_expert, n_used, xs, wg, wu, wd)


def _mlp_kernel(x_ref, wg_ref, wu_ref, wd_ref, o_ref):
    j = pl.program_id(1)
    x = x_ref[...]
    hg = jnp.dot(x, wg_ref[...], preferred_element_type=_F32)
    hu = jnp.dot(x, wu_ref[...], preferred_element_type=_F32)
    contrib = jnp.dot((jax.nn.silu(hg) * hu).astype(_BF16), wd_ref[...], preferred_element_type=_F32)

    @pl.when(j == 0)
    def _():
        o_ref[...] = contrib

    @pl.when(j > 0)
    def _():
        o_ref[...] += contrib


def _mlp(xb, wg, wu, wd, layer, tm, tf):
    m, d = xb.shape
    ff = wg.shape[2]
    tm, tf = min(tm, m), min(tf, ff)
    return pl.pallas_call(
        _mlp_kernel,
        out_shape=jax.ShapeDtypeStruct((m, d), _F32),
        grid=(m // tm, ff // tf),
        in_specs=[pl.BlockSpec((tm, d), lambda i, j: (i, 0)),
                  pl.BlockSpec((None, d, tf), lambda i, j: (layer, 0, j)),
                  pl.BlockSpec((None, d, tf), lambda i, j: (layer, 0, j)),
                  pl.BlockSpec((None, tf, d), lambda i, j: (layer, j, 0))],
        out_specs=pl.BlockSpec((tm, d), lambda i, j: (i, 0)),
        compiler_params=_params(2),
        name="shared_expert",
    )(xb, wg, wu, wd)


def _combine_kernel(alpha, pos_ref, pos_next_ref, w_ref, ys_ref, sh_ref, h_ref, g_ref, b_ref,
                    o32_ref, o16_ref, buf_ref, sem):
    tm = h_ref.shape[0]
    i = pl.program_id(0)
    slot = i % 2

    def gather(p_ref, into):
        def start(t, c):
            for k in range(_TOP_K):
                dst = (into * _TOP_K + k) * tm + t
                _slab_copy(ys_ref, p_ref[t * _TOP_K + k], buf_ref, dst, sem.at[into], _SLAB).start(priority=k % 2)
            return c
        lax.fori_loop(0, tm, start, 0)

    @pl.when(i == 0)
    def _():
        gather(pos_ref, 0)

    @pl.when(i + 1 < pl.num_programs(0))
    def _():
        gather(pos_next_ref, 1 - slot)

    _wait_slabs(ys_ref, buf_ref, _TOP_K * tm, sem.at[slot], _SLAB)

    acc_lo = acc_hi = None
    for k in range(_TOP_K):
        lo, hi = _unpack_halves(_load_slabs(buf_ref, (slot * _TOP_K + k) * tm, tm))
        wk = w_ref[:, k:k + 1]
        acc_lo = lo * wk if acc_lo is None else acc_lo + lo * wk
        acc_hi = hi * wk if acc_hi is None else acc_hi + hi * wk
    moe = jnp.concatenate([acc_lo, acc_hi], axis=1) + sh_ref[...]
    o = _layer_norm(alpha * h_ref[...] + moe, g_ref[...], b_ref[...])
    o32_ref[...] = o
    o16_ref[...] = o.astype(_BF16)


def _combine(ys, pos_flat, w_tok, shared, h, g, b, alpha):
    m, d = h.shape
    tm = _GATHER_TOKENS
    tile = pl.BlockSpec((tm, d), lambda i: (i, 0))
    row = pl.BlockSpec((1, d), lambda i: (0, 0))
    return pl.pallas_call(
        functools.partial(_combine_kernel, alpha),
        out_shape=(jax.ShapeDtypeStruct((m, d), _F32), jax.ShapeDtypeStruct((m, d), _BF16)),
        grid=(m // tm,),
        in_specs=[pl.BlockSpec((tm * _TOP_K,), lambda i: (i,), memory_space=pltpu.SMEM),
                  pl.BlockSpec((tm * _TOP_K,), lambda i: (jnp.minimum(i + 1, m // tm - 1),), memory_space=pltpu.SMEM),
                  pl.BlockSpec((tm, _TOP_K), lambda i: (i, 0)),
                  pl.BlockSpec(memory_space=pl.ANY),
                  tile, tile, row, row],
        out_specs=(tile, tile),
        scratch_shapes=[pltpu.VMEM((2 * _TOP_K * tm * _PITCH, _LANES), _U32), pltpu.SemaphoreType.DMA((2,))],
        compiler_params=_params(1),
        name="moe_combine_layernorm",
    )(pos_flat, pos_flat, w_tok, ys, shared, h, g, b)


def _routing_plan(e_t, r_t, counts, n_tiles):
    counts = counts.astype(jnp.int32)
    padded = (counts + _EXPERT_TILE - 1) // _EXPERT_TILE * _EXPERT_TILE
    ends = jnp.cumsum(padded)
    offsets = ends - padded
    experts = jnp.arange(_N_EXPERTS, dtype=jnp.int32)
    pos = r_t + jnp.sum(jnp.where(e_t[..., None] == experts, offsets, 0), axis=-1)
    tile_start = jnp.arange(n_tiles, dtype=jnp.int32) * _EXPERT_TILE
    tile_expert = jnp.sum((ends[None, :] <= tile_start[:, None]).astype(jnp.int32), axis=1)
    tile_expert = jnp.minimum(tile_expert, _N_EXPERTS - 1)
    n_used = (ends[-1:] // _EXPERT_TILE).astype(jnp.int32)
    half = _EXPERT_TILE // 2
    tail_start = ends[-1] + jnp.arange(2 * _N_EXPERTS, dtype=jnp.int32) * half
    tail_len = jnp.where(tail_start < n_tiles * _EXPERT_TILE, half, 0)
    pad_start = jnp.concatenate([offsets + counts, tail_start]).astype(jnp.int32)
    pad_len = jnp.concatenate([padded - counts, tail_len]).astype(jnp.int32)
    return pos.T.reshape(-1), tile_expert, n_used, pad_start, pad_len


def _ple_kernel(hb_ref, h_ref, p_ref, wpg_ref, wple_ref, o32_ref, o16_ref):
    gate = jax.nn.sigmoid(jnp.dot(hb_ref[...], wpg_ref[...], preferred_element_type=_F32))
    pe = jnp.dot(p_ref[...].astype(_BF16), wple_ref[...], preferred_element_type=_F32)
    o = h_ref[...] + gate * pe
    o32_ref[...] = o
    o16_ref[...] = o.astype(_BF16)


def _ple(hb, h, p, wpg, wple, layer, row0, rows, tm, tn):
    d = hb.shape[1]
    pd = p.shape[1]
    tm, tn = min(tm, rows), min(tn, d)
    i0 = row0 // tm
    assert row0 % tm == 0 and rows % tm == 0
    out_tile = pl.BlockSpec((tm, tn), lambda i, j: (i, j))
    return pl.pallas_call(
        _ple_kernel,
        out_shape=(jax.ShapeDtypeStruct((rows, d), _F32), jax.ShapeDtypeStruct((rows, d), _BF16)),
        grid=(rows // tm, d // tn),
        in_specs=[pl.BlockSpec((tm, d), lambda i, j: (i0 + i, 0)),
                  pl.BlockSpec((tm, tn), lambda i, j: (i0 + i, j)),
                  pl.BlockSpec((tm, pd), lambda i, j: (i0 + i, 0)),
                  pl.BlockSpec((None, d, tn), lambda i, j: (layer, 0, j)),
                  pl.BlockSpec((None, pd, tn), lambda i, j: (layer, 0, j))],
        out_specs=(out_tile, out_tile),
        compiler_params=_params(2),
        name="ple_gate",
    )(hb, h, p, wpg, wple)


def _layer(x, xb, p, image_rows, alpha, layer, out_rows, w, wb):
    (attn_rpb, gmlp_ln_g, gmlp_ln_b, gmlp_bs, mix_norm_g, ln1_g, ln1_b, w_router, router_bias, ln2_g, ln2_b) = w
    (w_in, gmlp_ws, w_out, w_e_gate, w_e_up, w_e_down, w_sh_gate, w_sh_up, w_sh_down, w_ple, w_ple_gate) = wb
    row = lambda v: v.reshape(1, -1).astype(_F32)

    proj = _matmul(xb, w_in, layer, _BF16, 1024, 1024, "in_proj")

    rs_tbl, d0_tbl = _attn_tables(image_rows)
    a = _attention(proj, _attn_bias_table(attn_rpb), row(mix_norm_g[:_ATTN_W]),
                   jnp.asarray(rs_tbl), jnp.asarray(d0_tbl))

    bsx = jnp.repeat(gmlp_bs.T.astype(_F32), _HEAD_DIM, axis=1)
    g = _gmlp(proj, gmlp_ws[layer], row(gmlp_ln_g), row(gmlp_ln_b), bsx, row(mix_norm_g[_ATTN_W:]))

    mixed_out = _out_proj(a, g, w_out, layer, 1024, 1024)
    h, hb, hpk = _add_ln(x, mixed_out, row(ln1_g), row(ln1_b), alpha, 256)

    wr_p = jnp.pad(w_router.astype(_F32), ((0, 0), (0, _LANES - _N_EXPERTS)))
    bias_p = jnp.broadcast_to(router_bias.astype(_F32)[:, None], (_N_EXPERTS, _LANES))
    e_t, w_t, r_t, counts = _router(h, wr_p, bias_p, 512)

    n_rows = x.shape[0] * _TOP_K + _N_EXPERTS * _EXPERT_TILE
    pos_flat, tile_expert, n_used, pad_start, pad_len = _routing_plan(
        e_t, r_t, counts[:, 0], n_rows // _EXPERT_TILE)
    xs = _dispatch(hpk, pos_flat, pad_start, pad_len, n_rows)
    ys = _experts(xs, w_e_gate, w_e_up, w_e_down, layer, tile_expert, n_used)
    shared = _mlp(hb, w_sh_gate, w_sh_up, w_sh_down, layer, 512, 512)
    h2, h2b = _combine(ys, pos_flat, w_t.T, shared, h, row(ln2_g), row(ln2_b), alpha)
    return [_ple(h2b, h2, p, w_ple_gate, w_ple, layer, r0, nr, 1024, 512) for r0, nr in out_rows]


def kernel(x_prompt, x_sample, p_prompt, p_sample, w_in, attn_rpb, gmlp_ln_g, gmlp_ln_b, gmlp_ws, gmlp_bs,
           mix_norm_g, w_out, ln1_g, ln1_b, w_router, router_bias, w_e_gate, w_e_up, w_e_down,
           w_sh_gate, w_sh_up, w_sh_down, ln2_g, ln2_b, w_ple, w_ple_gate):
    depth = w_in.shape[0]
    d = x_prompt.shape[-1]
    alpha = (2 * depth) ** 0.25
    groups = (x_prompt, x_sample)
    image_rows = [xg.shape[1] // _GRID_W for xg in groups for _ in range(xg.shape[0])]
    x = jnp.concatenate([xg.reshape(-1, d) for xg in groups], axis=0)
    xb = x.astype(_BF16)
    n, n_prompt = x.shape[0], x_prompt.shape[0] * x_prompt.shape[1]
    small = (attn_rpb, gmlp_ln_g, gmlp_ln_b, gmlp_bs, mix_norm_g, ln1_g, ln1_b, w_router, router_bias, ln2_g, ln2_b)
    wb = tuple(wt.astype(_BF16) for wt in (w_in, gmlp_ws, w_out, w_e_gate, w_e_up, w_e_down,
                                           w_sh_gate, w_sh_up, w_sh_down, w_ple, w_ple_gate))
    for i in range(depth):
        p = jnp.concatenate([pg[i].reshape(-1, pg.shape[-1]) for pg in (p_prompt, p_sample)], axis=0)
        last = i == depth - 1
        out_rows = [(0, n_prompt), (n_prompt, n - n_prompt)] if last else [(0, n)]
        outs = _layer(x, xb, p, image_rows, alpha, i, out_rows, tuple(wt[i] for wt in small), wb)
        x, xb = outs[0]
    return outs[0][0].reshape(x_prompt.shape), outs[1][0].reshape(x_sample.shape)
```
